```python
import math
import jax, jax.numpy as jnp
from jax import lax
import numpy as np

D_MODEL = 1024
BATCH = 4
SEQ = 4096
DEPTH = 1

HEAD_DIM = 64
N_FOX_HEADS = 8
N_DSA_HEADS = 8
N_IDX_HEADS = 4
IDX_DIM = 64
TOPK_MAX = 256
Q_BLOCK = 128
N_BUCKETS = 32
MAX_DISTANCE = 128
D_FF = 4 * D_MODEL
EPS = 1e-6

FOX_W = N_FOX_HEADS * HEAD_DIM
DSA_W = N_DSA_HEADS * HEAD_DIM
SPLIT_SIZES = [
    FOX_W, FOX_W, FOX_W,
    N_FOX_HEADS,
    DSA_W, DSA_W, DSA_W,
    N_IDX_HEADS * IDX_DIM,
    IDX_DIM,
    N_IDX_HEADS,
    D_MODEL, D_MODEL,
]
D_IN = int(sum(SPLIT_SIZES))
SPLIT_POINTS = [int(v) for v in np.cumsum(SPLIT_SIZES)[:-1]]

kernel_name = "hybrid_fox_dsa_gated_block"


def rmsnorm(x, g):
    xf = x.astype(jnp.float32)
    y = xf * lax.rsqrt(jnp.mean(xf * xf, axis=-1, keepdims=True) + EPS)
    return (y * g.astype(jnp.float32)).astype(x.dtype)


def t5_bucket(dist):
    n = jnp.maximum(dist, 0)
    max_exact = N_BUCKETS // 2
    is_small = n < max_exact
    nf = jnp.maximum(n, 1).astype(jnp.float32)
    large = max_exact + (jnp.log(nf / max_exact) / math.log(MAX_DISTANCE / max_exact)
                         * (N_BUCKETS - max_exact)).astype(jnp.int32)
    large = jnp.minimum(large, N_BUCKETS - 1)
    return jnp.where(is_small, n, large)


def forgetting_attention(q, k, v, cum_logf):
    B, S, H, Dh = q.shape
    n_blocks = S // Q_BLOCK
    scale = Dh ** -0.5
    Ft = jnp.transpose(cum_logf, (0, 2, 1))
    key_pos = jnp.arange(S)

    def block(i):
        start = i * Q_BLOCK
        qb = lax.dynamic_slice_in_dim(q, start, Q_BLOCK, axis=1)
        Fb = lax.dynamic_slice_in_dim(Ft, start, Q_BLOCK, axis=2)
        s = jnp.einsum('bqhd,bkhd->bhqk', qb, k).astype(jnp.float32) * scale
        s = s + Fb[..., :, None] - Ft[:, :, None, :]
        q_pos = start + jnp.arange(Q_BLOCK)
        mask = key_pos[None, :] <= q_pos[:, None]
        s = jnp.where(mask[None, None], s, -jnp.inf)
        p = jax.nn.softmax(s, axis=-1).astype(v.dtype)
        return jnp.einsum('bhqk,bkhd->bqhd', p, v)

    out = lax.map(block, jnp.arange(n_blocks))
    return jnp.transpose(out, (1, 0, 2, 3, 4)).reshape(B, S, H, Dh)


def indexer_sparse_attention(q, k, v, q_idx, k_idx, w_idx, rel_bias):
    B, S, H, Dh = q.shape
    n_blocks = S // Q_BLOCK
    topk = min(TOPK_MAX, S // 4)
    scale = Dh ** -0.5
    idx_scale = IDX_DIM ** -0.5
    key_pos = jnp.arange(S)
    gather = jax.vmap(lambda arr, ids: arr[ids])

    def block(i):
        start = i * Q_BLOCK
        qb = lax.dynamic_slice_in_dim(q, start, Q_BLOCK, axis=1)
        qib = lax.dynamic_slice_in_dim(q_idx, start, Q_BLOCK, axis=1)
        wib = lax.dynamic_slice_in_dim(w_idx, start, Q_BLOCK, axis=1)
        q_pos = start + jnp.arange(Q_BLOCK)
        sc = jnp.einsum('bqhd,bkd->bqhk', qib, k_idx).astype(jnp.float32) * idx_scale
        iscore = jnp.einsum('bqhk,bqh->bqk', jax.nn.relu(sc), wib.astype(jnp.float32))
        causal = key_pos[None, :] <= q_pos[:, None]
        iscore = jnp.where(causal[None], iscore, -jnp.inf)
        _, sel = lax.top_k(iscore, topk)
        kg = gather(k, sel)
        vg = gather(v, sel)
        logits = jnp.einsum('bqhd,bqkhd->bhqk', qb, kg).astype(jnp.float32) * scale
        dist = q_pos[None, :, None] - sel
        valid = dist >= 0
        bias = rel_bias[t5_bucket(dist)].astype(jnp.float32)
        logits = logits + jnp.transpose(bias, (0, 3, 1, 2))
        logits = jnp.where(valid[:, None], logits, -jnp.inf)
        p = jax.nn.softmax(logits, axis=-1).astype(vg.dtype)
        return jnp.einsum('bhqk,bqkhd->bqhd', p, vg)

    out = lax.map(block, jnp.arange(n_blocks))
    return jnp.transpose(out, (1, 0, 2, 3, 4)).reshape(B, S, H, Dh)


def setup_inputs(seed: int = 0) -> dict:
    key = jax.random.key(seed)
    ks = jax.random.split(key, 16)
    D = D_MODEL
    nrm = lambda k, shape, s: jax.random.normal(k, shape, jnp.float32) * s
    return {
        "x": nrm(ks[0], (BATCH, SEQ, D), 1.0),
        "c": nrm(ks[1], (BATCH, D), 1.0),
        "w_ada": nrm(ks[2], (D, 6 * D), 0.5 * D ** -0.5),
        "b_ada": nrm(ks[3], (6 * D,), 0.02),
        "g_norm1": 1.0 + nrm(ks[4], (D,), 0.02),
        "w_in": nrm(ks[5], (D, D_IN), D ** -0.5),
        "b_forget": nrm(ks[6], (N_FOX_HEADS,), 0.1),
        "rel_bias": nrm(ks[7], (N_BUCKETS, N_DSA_HEADS), 0.5),
        "w_branch_fox": nrm(ks[8], (FOX_W, D), FOX_W ** -0.5),
        "w_branch_dsa": nrm(ks[9], (DSA_W, D), DSA_W ** -0.5),
        "w_out": nrm(ks[10], (D, D), D ** -0.5),
        "g_norm2": 1.0 + nrm(ks[11], (D,), 0.02),
        "w_mlp1": nrm(ks[12], (D, D_FF), D ** -0.5),
        "w_mlp2": nrm(ks[13], (D_FF, D), D_FF ** -0.5),
        "g_final": 1.0 + nrm(ks[14], (D,), 0.02),
    }


def reference(x, c, w_ada, b_ada, g_norm1, w_in, b_forget, rel_bias, w_branch_fox,
              w_branch_dsa, w_out, g_norm2, w_mlp1, w_mlp2, g_final):
    B, S, D = x.shape
    ada = (c @ w_ada + b_ada)[:, None, :]
    shift1, scale1, gate1, shift2, scale2, gate2 = jnp.split(ada, 6, axis=-1)

    for _ in range(DEPTH):
        h = rmsnorm(x, g_norm1) * (1.0 + scale1) + shift1
        proj = h @ w_in
        (q_f, k_f, v_f, f_logit, q_d, k_d, v_d, q_i, k_i, w_i,
         gate_fox, gate_dsa) = jnp.split(proj, SPLIT_POINTS, axis=-1)
        hs = (B, S, -1, HEAD_DIM)
        log_f = jax.nn.log_sigmoid((f_logit + b_forget).astype(jnp.float32))
        cum_logf = jnp.cumsum(log_f, axis=1)
        y_fox = forgetting_attention(q_f.reshape(hs), k_f.reshape(hs), v_f.reshape(hs),
                                     cum_logf).reshape(B, S, FOX_W)
        w_i = w_i * (N_IDX_HEADS ** -0.5)
        y_dsa = indexer_sparse_attention(
            q_d.reshape(hs), k_d.reshape(hs), v_d.reshape(hs),
            q_i.reshape(B, S, N_IDX_HEADS, IDX_DIM), k_i, w_i, rel_bias
        ).reshape(B, S, DSA_W)
        merged = (jax.nn.sigmoid(gate_fox) * (y_fox @ w_branch_fox)
                  + jax.nn.sigmoid(gate_dsa) * (y_dsa @ w_branch_dsa))
        x = x + gate1 * (merged @ w_out)

        h2 = rmsnorm(x, g_norm2) * (1.0 + scale2) + shift2
        x = x + gate2 * (jnp.square(jax.nn.relu(h2 @ w_mlp1)) @ w_mlp2)

    return rmsnorm(x, g_final)
```

```python
import functools
import math

import jax
import jax.numpy as jnp
from jax import lax
from jax.experimental import pallas as pl
from jax.experimental.pallas import tpu as pltpu

F32 = jnp.float32
BF16 = jnp.bfloat16
I32 = jnp.int32

HEAD_DIM = 64
N_FOX_HEADS = 8
N_DSA_HEADS = 8
N_IDX_HEADS = 4
IDX_DIM = 64
TOPK_MAX = 256
N_BUCKETS = 32
MAX_DISTANCE = 128
EPS = 1e-6

LANES = 128
INT_MIN = -2 ** 31
NEG = -1e30
VMEM_LIMIT = 52 * 1024 * 1024

PROJ_TM = 512
ATT_T = 256
POST_TM = 256
FF_CHUNK = 1024
CUM_CHUNK = 512
ROW_GROUP = 64

_NT = (((1,), (1,)), ((), ()))


def _resident(block_shape, index_map):
    return pl.BlockSpec(block_shape, index_map, pipeline_mode=pl.Buffered(1))


def _ada_kernel(c_ref, w_ref, b_ref, o_ref):
    o_ref[...] = jnp.dot(c_ref[...], w_ref[...], preferred_element_type=F32,
                         precision=lax.Precision.HIGHEST) + b_ref[...]


def _ada(c, w_ada, b_ada):
    B, D = c.shape
    n = w_ada.shape[1]
    rows = 8
    c_pad = jnp.zeros((rows, D), F32).at[:B].set(c)
    out = pl.pallas_call(
        _ada_kernel,
        out_shape=jax.ShapeDtypeStruct((rows, n), F32),
        grid=(n // D,),
        in_specs=[pl.BlockSpec((rows, D), lambda j: (0, 0)),
                  pl.BlockSpec((D, D), lambda j: (0, j)),
                  pl.BlockSpec((1, D), lambda j: (0, j))],
        out_specs=pl.BlockSpec((rows, D), lambda j: (0, j)),
        compiler_params=pltpu.CompilerParams(vmem_limit_bytes=VMEM_LIMIT),
        name="ada",
    )(c_pad, w_ada, b_ada.reshape(1, n))
    return out[:B].reshape(B, n // D, D)


def _biastab_kernel(rb_ref, o_ref):
    h = pl.program_id(0)
    e = pl.program_id(1)
    t = o_ref.shape[-1]
    ii = lax.broadcasted_iota(I32, (t, t), 0)
    jj = lax.broadcasted_iota(I32, (t, t), 1)
    n = jnp.maximum(ii - jj + e * t, 0)
    max_exact = N_BUCKETS // 2
    nf = jnp.maximum(n, 1).astype(F32)
    large = max_exact + (jnp.log(nf / max_exact) / math.log(MAX_DISTANCE / max_exact)
                         * (N_BUCKETS - max_exact)).astype(I32)
    large = jnp.minimum(large, N_BUCKETS - 1)
    bucket = jnp.where(n < max_exact, n, large)
    acc = jnp.zeros((t, t), F32)
    for b in range(N_BUCKETS):
        acc = jnp.where(bucket == b, rb_ref[b, h], acc)
    o_ref[0, 0] = acc - rb_ref[N_BUCKETS - 1, h]


def _biastab(rel_bias, t):
    nh = rel_bias.shape[1]
    return pl.pallas_call(
        _biastab_kernel,
        out_shape=jax.ShapeDtypeStruct((nh, 2, t, t), F32),
        grid=(nh, 2),
        in_specs=[pl.BlockSpec(memory_space=pltpu.SMEM)],
        out_specs=pl.BlockSpec((1, 1, t, t), lambda h, e: (h, e, 0, 0)),
        name="biastab",
    )(rel_bias)


def _rms_mod(x, g, scale, shift):
    y = x * lax.rsqrt(jnp.mean(x * x, axis=-1, keepdims=True) + EPS) * g
    return y * (1.0 + scale) + shift


def _proj_kernel(x_ref, ada_ref, g_ref, w_ref, ws_ref,
                 qf_ref, kf_ref, vf_ref, qd_ref, kd_ref, vd_ref, qi_ref, kk_ref, wi_ref,
                 sgf_ref, sgd_ref, ft_ref):
    h = _rms_mod(x_ref[...], g_ref[...], ada_ref[0, 1:2, :], ada_ref[0, 0:1, :])
    hb = h.astype(BF16)
    col = 0
    for ref in (qf_ref, kf_ref, vf_ref, qd_ref, kd_ref, vd_ref, qi_ref, kk_ref, wi_ref):
        n = ref.shape[-1]
        r = jnp.dot(hb, w_ref[:, col:col + n], preferred_element_type=F32)
        ref[...] = r.astype(ref.dtype)
        col += n
    for ref in (sgf_ref, sgd_ref):
        n = ref.shape[-1]
        r = jnp.dot(hb, w_ref[:, col:col + n], preferred_element_type=F32)
        ref[...] = (1.0 / (1.0 + jnp.exp(-r))).astype(ref.dtype)
        col += n
    ft = lax.dot_general(ws_ref[...], hb, _NT, preferred_element_type=F32)
    ft_ref[0] = ft[0:N_FOX_HEADS]


def _proj(x2d, ada, g1, w_main, w_small_t, B, S):
    M, D = x2d.shape
    tm = PROJ_TM
    tpb = S // tm
    fw = N_FOX_HEADS * HEAD_DIM
    iw = N_IDX_HEADS * IDX_DIM
    widths = [fw] * 6 + [iw, LANES, LANES]
    dtypes = [BF16] * 8 + [F32]
    out_shape = [jax.ShapeDtypeStruct((M, w), dt) for w, dt in zip(widths, dtypes)]
    out_specs = [pl.BlockSpec((tm, w), lambda i: (i, 0)) for w in widths]
    out_shape += [jax.ShapeDtypeStruct((M, D), BF16)] * 2
    out_specs += [pl.BlockSpec((tm, D), lambda i: (i, 0))] * 2
    out_shape.append(jax.ShapeDtypeStruct((B, N_FOX_HEADS, S), F32))
    out_specs.append(pl.BlockSpec((1, N_FOX_HEADS, tm), lambda i: (i // tpb, 0, i % tpb)))
    return pl.pallas_call(
        _proj_kernel,
        out_shape=out_shape,
        grid=(M // tm,),
        in_specs=[pl.BlockSpec((tm, D), lambda i: (i, 0)),
                  pl.BlockSpec((1,) + ada.shape[1:], lambda i: (i // tpb, 0, 0)),
                  pl.BlockSpec((1, D), lambda i: (0, 0)),
                  _resident(w_main.shape, lambda i: (0, 0)),
                  _resident(w_small_t.shape, lambda i: (0, 0))],
        out_specs=out_specs,
        compiler_params=pltpu.CompilerParams(dimension_semantics=("arbitrary",),
                                             vmem_limit_bytes=VMEM_LIMIT),
        name="proj",
    )(x2d, ada, g1, w_main, w_small_t)


def _fcum_kernel(f_ref, b_ref, o_ref):
    x = f_ref[0] + b_ref[...]
    logf = -(jnp.maximum(-x, 0.0) + jnp.log1p(jnp.exp(-jnp.abs(x))))
    ch = CUM_CHUNK
    r = lax.broadcasted_iota(I32, (ch, ch), 0)
    c = lax.broadcasted_iota(I32, (ch, ch), 1)
    upper = (r <= c).astype(F32)
    carry = jnp.zeros((x.shape[0], 1), F32)
    for k in range(x.shape[1] // ch):
        seg = logf[:, k * ch:(k + 1) * ch]
        cs = jnp.dot(seg, upper, preferred_element_type=F32,
                     precision=lax.Precision.HIGHEST) + carry
        o_ref[0, :, k * ch:(k + 1) * ch] = cs
        carry = cs[:, ch - 1:ch]


def _fcum(ft, b_forget):
    B, H, S = ft.shape
    return pl.pallas_call(
        _fcum_kernel,
        out_shape=jax.ShapeDtypeStruct((B, H, S), F32),
        grid=(B,),
        in_specs=[pl.BlockSpec((1, H, S), lambda b: (b, 0, 0)),
                  pl.BlockSpec((H, 1), lambda b: (0, 0))],
        out_specs=pl.BlockSpec((1, H, S), lambda b: (b, 0, 0)),
        name="fcum",
    )(ft, b_forget.reshape(H, 1))


def _online_update(s, vt, m_ref, l_ref, acc_ref, h):
    m_prev = m_ref[h]
    m_new = jnp.maximum(m_prev, jnp.max(s, axis=1, keepdims=True))
    alpha = jnp.exp(m_prev - m_new)
    p = jnp.exp(s - m_new)
    l_ref[h] = alpha * l_ref[h] + jnp.sum(p, axis=1, keepdims=True)
    acc_ref[h] = alpha * acc_ref[h] + jnp.dot(p.astype(BF16), vt, preferred_element_type=F32)
    m_ref[h] = m_new


def _init_softmax(m_ref, l_ref, acc_ref):
    m_ref[...] = jnp.full(m_ref.shape, NEG, F32)
    l_ref[...] = jnp.zeros(l_ref.shape, F32)
    acc_ref[...] = jnp.zeros(acc_ref.shape, F32)


def _split_pair(qp, lane):
    qf = qp.astype(F32)
    return (jnp.where(lane < HEAD_DIM, qf, 0.0).astype(qp.dtype),
            jnp.where(lane >= HEAD_DIM, qf, 0.0).astype(qp.dtype))


def _fox_kernel(q_ref, k_ref, v_ref, f_ref, o_ref, acc_ref, m_ref, l_ref):
    j = pl.program_id(1)
    i = pl.program_id(2)
    t = q_ref.shape[1]
    lane = lax.broadcasted_iota(I32, (t, LANES), 1)
    qm = _split_pair(q_ref[0], lane)
    _init_softmax(m_ref, l_ref, acc_ref)
    row = lax.broadcasted_iota(I32, (t, t), 0)
    col = lax.broadcasted_iota(I32, (t, t), 1)

    def tile(kt, masked):
        off = pl.multiple_of(kt * t, t)
        ks = k_ref[0, pl.ds(off, t), :]
        vs = v_ref[0, pl.ds(off, t), :]
        for e in range(2):
            fs = f_ref[0, 2 * j + e, pl.ds(kt, 1), :]
            s = lax.dot_general(qm[e], ks, _NT, preferred_element_type=F32) - fs
            if masked:
                s = jnp.where(col <= row, s, NEG)
            _online_update(s, vs, m_ref, l_ref, acc_ref, e)

    def body(kt, carry):
        tile(kt, False)
        return carry

    lax.fori_loop(0, i, body, 0)
    tile(i, True)
    out = jnp.where(lane < HEAD_DIM, acc_ref[0] / l_ref[0], acc_ref[1] / l_ref[1])
    o_ref[0] = out.astype(o_ref.dtype)


def _fox(qf, kf, vf, f4):
    B, S, W = qf.shape
    t = ATT_T
    npair = W // LANES
    nk = S // t
    return pl.pallas_call(
        _fox_kernel,
        out_shape=jax.ShapeDtypeStruct((B, S, W), BF16),
        grid=(B, npair, S // t),
        in_specs=[pl.BlockSpec((1, t, LANES), lambda b, j, i: (b, i, j)),
                  pl.BlockSpec((1, S, LANES), lambda b, j, i: (b, 0, j)),
                  pl.BlockSpec((1, S, LANES), lambda b, j, i: (b, 0, j)),
                  pl.BlockSpec((1, N_FOX_HEADS, nk, t), lambda b, j, i: (b, 0, 0, 0))],
        out_specs=pl.BlockSpec((1, t, LANES), lambda b, j, i: (b, i, j)),
        scratch_shapes=[pltpu.VMEM((2, t, LANES), F32),
                        pltpu.VMEM((2, t, 1), F32),
                        pltpu.VMEM((2, t, 1), F32)],
        compiler_params=pltpu.CompilerParams(
            dimension_semantics=("arbitrary", "arbitrary", "arbitrary"),
            vmem_limit_bytes=VMEM_LIMIT),
        name="fox",
    )(qf, kf, vf, f4)


def _dsa_kernel(qd_ref, qi_ref, wi_ref, kd_ref, vd_ref, kk_ref, tab_ref, o_ref,
                keys_ref, qm_ref, qim_ref, tau_ref, pos_ref, acc_ref, m_ref, l_ref):
    i = pl.program_id(1)
    t = qd_ref.shape[1]
    S = kd_ref.shape[1]
    topk = min(TOPK_MAX, S // 4)
    lane = lax.broadcasted_iota(I32, (t, LANES), 1)
    row = lax.broadcasted_iota(I32, (t, t), 0)
    col = lax.broadcasted_iota(I32, (t, t), 1)

    for p in range(N_DSA_HEADS // 2):
        a, b = _split_pair(qd_ref[0, :, p * LANES:(p + 1) * LANES], lane)
        qm_ref[2 * p] = a
        qm_ref[2 * p + 1] = b
    for p in range(N_IDX_HEADS // 2):
        a, b = _split_pair(qi_ref[0, :, p * LANES:(p + 1) * LANES], lane)
        qim_ref[2 * p] = a
        qim_ref[2 * p + 1] = b
    w = wi_ref[0][:, 0:N_IDX_HEADS] * (N_IDX_HEADS ** -0.5)
    wcols = [w[:, h:h + 1] for h in range(N_IDX_HEADS)]

    def score_chunk(c, diag):
        kt = kk_ref[0, pl.ds(pl.multiple_of(c * t, t), t), :]
        sc = None
        for h in range(N_IDX_HEADS):
            d = lax.dot_general(qim_ref[h], kt, _NT, preferred_element_type=F32)
            term = jnp.maximum(d, 0.0) * wcols[h]
            sc = term if sc is None else sc + term
        sc = jnp.where(sc == 0.0, 0.0, sc)
        bits = lax.bitcast_convert_type(sc, I32)
        key = jnp.where(bits < 0, bits ^ 0x7FFFFFFF, bits)
        if diag:
            key = jnp.where(col <= row, key, INT_MIN)
        keys_ref[c] = key

    def score_body(c, carry):
        score_chunk(c, False)
        return carry

    lax.fori_loop(0, i, score_body, 0)
    score_chunk(i, True)
    nch = i + 1

    ng = t // ROW_GROUP
    col_g = lax.broadcasted_iota(I32, (ROW_GROUP, t), 1)

    def count(g, pred):
        def body(c, acc):
            x = keys_ref[c, g * ROW_GROUP:(g + 1) * ROW_GROUP, :]
            colg = col_g + c * t
            v = jnp.where(pred(x, colg), 1.0, 0.0)
            for k in range(t // LANES):
                acc = acc + v[:, k * LANES:(k + 1) * LANES]
            return acc
        acc = lax.fori_loop(0, nch, body, jnp.zeros((ROW_GROUP, LANES), F32))
        return jnp.sum(acc, axis=1, keepdims=True)

    def tau_step(jj, prefixes):
        bit = jnp.left_shift(jnp.int32(1), 31 - jj)
        out = []
        for g in range(ng):
            cand = prefixes[g] + bit
            cnt = count(g, lambda x, colg: x >= cand)
            out.append(jnp.where(cnt >= topk, cand, prefixes[g]))
        return tuple(out)

    taus = lax.fori_loop(0, 32, tau_step,
                         tuple(jnp.full((ROW_GROUP, 1), INT_MIN, I32) for _ in range(ng)))

    needs = []
    bad = jnp.zeros((ROW_GROUP, 1), I32)
    for g in range(ng):
        tau = taus[g]
        tau_ref[g * ROW_GROUP:(g + 1) * ROW_GROUP, :] = tau
        c_gt = count(g, lambda x, colg: x > tau)
        c_ge = count(g, lambda x, colg: x >= tau)
        needs.append(topk - c_gt)
        surplus = jnp.logical_and(c_ge != topk, tau != INT_MIN)
        bad = jnp.maximum(bad, surplus.astype(I32))
    pos_ref[...] = jnp.full(pos_ref.shape, S, I32)

    @pl.when(jnp.max(bad) > 0)
    def _():
        nbits = int(math.log2(S))

        def pos_step(jj, ps):
            bit = jnp.left_shift(jnp.int32(1), nbits - 1 - jj)
            out = []
            for g in range(ng):
                tau = tau_ref[g * ROW_GROUP:(g + 1) * ROW_GROUP, :]
                cand = ps[g] + bit
                cnt = count(g, lambda x, colg: jnp.logical_and(x == tau, colg < cand))
                out.append(jnp.where(cnt < needs[g], cand, ps[g]))
            return tuple(out)

        ps = lax.fori_loop(0, nbits, pos_step,
                           tuple(jnp.zeros((ROW_GROUP, 1), I32) for _ in range(ng)))
        for g in range(ng):
            pos_ref[g * ROW_GROUP:(g + 1) * ROW_GROUP, :] = ps[g]

    _init_softmax(m_ref, l_ref, acc_ref)

    def attend(c, mode):
        key = keys_ref[c]
        tau = tau_ref[...]
        pos = pos_ref[...]
        colg = col + c * t
        keep_tie = jnp.where(colg <= pos, 0.0, NEG)
        maskadd = jnp.where(key > tau, 0.0, jnp.where(key == tau, keep_tie, NEG))
        if mode == 0:
            maskadd = jnp.where(col <= row, maskadd, NEG)
        off = pl.multiple_of(c * t, t)
        for h in range(N_DSA_HEADS):
            p = h // 2
            kt = kd_ref[0, pl.ds(off, t), p * LANES:(p + 1) * LANES]
            vt = vd_ref[0, pl.ds(off, t), p * LANES:(p + 1) * LANES]
            s = lax.dot_general(qm_ref[h], kt, _NT, preferred_element_type=F32) + maskadd
            if mode < 2:
                s = s + tab_ref[h, mode]
            _online_update(s, vt, m_ref, l_ref, acc_ref, h)

    def far_body(c, carry):
        attend(c, 2)
        return carry

    lax.fori_loop(0, jnp.maximum(i - 1, 0), far_body, 0)

    @pl.when(i >= 1)
    def _():
        attend(i - 1, 1)

    attend(i, 0)
    for p in range(N_DSA_HEADS // 2):
        out = jnp.where(lane < HEAD_DIM, acc_ref[2 * p] / l_ref[2 * p],
                        acc_ref[2 * p + 1] / l_ref[2 * p + 1])
        o_ref[0, :, p * LANES:(p + 1) * LANES] = out.astype(o_ref.dtype)


def _dsa(qd, qi, wi, kd, vd, kk, tab):
    B, S, W = qd.shape
    t = ATT_T
    nk = S // t
    return pl.pallas_call(
        _dsa_kernel,
        out_shape=jax.ShapeDtypeStruct((B, S, W), BF16),
        grid=(B, nk),
        in_specs=[pl.BlockSpec((1, t, W), lambda b, i: (b, i, 0)),
                  pl.BlockSpec((1, t, qi.shape[2]), lambda b, i: (b, i, 0)),
                  pl.BlockSpec((1, t, LANES), lambda b, i: (b, i, 0)),
                  _resident((1, S, W), lambda b, i: (b, 0, 0)),
                  _resident((1, S, W), lambda b, i: (b, 0, 0)),
                  _resident((1, S, LANES), lambda b, i: (b, 0, 0)),
                  _resident(tab.shape, lambda b, i: (0, 0, 0, 0))],
        out_specs=pl.BlockSpec((1, t, W), lambda b, i: (b, i, 0)),
        scratch_shapes=[pltpu.VMEM((nk, t, t), I32),
                        pltpu.VMEM((N_DSA_HEADS, t, LANES), BF16),
                        pltpu.VMEM((N_IDX_HEADS, t, LANES), BF16),
                        pltpu.VMEM((t, 1), I32),
                        pltpu.VMEM((t, 1), I32),
                        pltpu.VMEM((N_DSA_HEADS, t, LANES), F32),
                        pltpu.VMEM((N_DSA_HEADS, t, 1), F32),
                        pltpu.VMEM((N_DSA_HEADS, t, 1), F32)],
        compiler_params=pltpu.CompilerParams(
            dimension_semantics=("arbitrary", "arbitrary"),
            vmem_limit_bytes=VMEM_LIMIT),
        name="dsa",
    )(qd, qi, wi, kd, vd, kk, tab)


def _post_kernel(x_ref, yf_ref, yd_ref, sgf_ref, sgd_ref, ada_ref, g2_ref, gf_ref,
                 wbf_ref, wbd_ref, wo_ref, w1_ref, w2_ref, o_ref):
    bf = jnp.dot(yf_ref[...], wbf_ref[...], preferred_element_type=F32)
    bd = jnp.dot(yd_ref[...], wbd_ref[...], preferred_element_type=F32)
    merged = sgf_ref[...].astype(F32) * bf + sgd_ref[...].astype(F32) * bd
    o = jnp.dot(merged.astype(BF16), wo_ref[...], preferred_element_type=F32)
    x1 = x_ref[...] + ada_ref[0, 2:3, :] * o
    h2 = _rms_mod(x1, g2_ref[...], ada_ref[0, 4:5, :], ada_ref[0, 3:4, :]).astype(BF16)
    acc = jnp.zeros(x1.shape, F32)
    for k in range(w1_ref.shape[1] // FF_CHUNK):
        u = jnp.dot(h2, w1_ref[:, k * FF_CHUNK:(k + 1) * FF_CHUNK], preferred_element_type=F32)
        u = jnp.square(jnp.maximum(u, 0.0)).astype(BF16)
        acc = acc + jnp.dot(u, w2_ref[k * FF_CHUNK:(k + 1) * FF_CHUNK, :],
                            preferred_element_type=F32)
    x2 = x1 + ada_ref[0, 5:6, :] * acc
    y = x2 * lax.rsqrt(jnp.mean(x2 * x2, axis=-1, keepdims=True) + EPS) * gf_ref[...]
    o_ref[...] = y


def _post(x2d, yf, yd, sgf, sgd, ada, g2, gfin, wbf, wbd, wo, w1, w2, S):
    M, D = x2d.shape
    tm = POST_TM
    tpb = S // tm
    rows = lambda w: pl.BlockSpec((tm, w), lambda i: (i, 0))
    full = lambda a: _resident(a.shape, lambda i: (0,) * a.ndim)
    return pl.pallas_call(
        _post_kernel,
        out_shape=jax.ShapeDtypeStruct((M, D), F32),
        grid=(M // tm,),
        in_specs=[rows(D), rows(yf.shape[1]), rows(yd.shape[1]), rows(D), rows(D),
                  pl.BlockSpec((1,) + ada.shape[1:], lambda i: (i // tpb, 0, 0)),
                  pl.BlockSpec((1, D), lambda i: (0, 0)),
                  pl.BlockSpec((1, D), lambda i: (0, 0)),
                  full(wbf), full(wbd), full(wo), full(w1), full(w2)],
        out_specs=rows(D),
        compiler_params=pltpu.CompilerParams(dimension_semantics=("arbitrary",),
                                             vmem_limit_bytes=VMEM_LIMIT),
        name="post",
    )(x2d, yf, yd, sgf, sgd, ada, g2, gfin, wbf, wbd, wo, w1, w2)


def kernel(x, c, w_ada, b_ada, g_norm1, w_in, b_forget, rel_bias, w_branch_fox,
           w_branch_dsa, w_out, g_norm2, w_mlp1, w_mlp2, g_final):
    B, S, D = x.shape
    fw = N_FOX_HEADS * HEAD_DIM
    dw = N_DSA_HEADS * HEAD_DIM
    iw = N_IDX_HEADS * IDX_DIM
    sizes = [fw, fw, fw, N_FOX_HEADS, dw, dw, dw, iw, IDX_DIM, N_IDX_HEADS, D, D]
    offs = [0]
    for s_ in sizes:
        offs.append(offs[-1] + s_)
    seg = lambda k: w_in[:, offs[k]:offs[k + 1]]
    q_scale = HEAD_DIM ** -0.5
    i_scale = IDX_DIM ** -0.5
    w_main = jnp.concatenate(
        [seg(0) * q_scale, seg(1), seg(2), seg(4) * q_scale, seg(5), seg(6),
         seg(7) * i_scale, seg(8), seg(8),
         seg(9), jnp.zeros((D, LANES - N_IDX_HEADS), F32), seg(10), seg(11)],
        axis=1).astype(BF16)
    w_small_t = jnp.zeros((16, D), F32).at[:N_FOX_HEADS].set(seg(3).T).astype(BF16)

    ada = _ada(c, w_ada, b_ada)
    tab = _biastab(rel_bias, ATT_T)
    x2d = x.reshape(B * S, D)
    (qf, kf, vf, qd, kd, vd, qi, kk, wi, sgf, sgd, ft) = _proj(
        x2d, ada, g_norm1.reshape(1, D), w_main, w_small_t, B, S)
    fcum = _fcum(ft, b_forget)
    r3 = lambda a: a.reshape(B, S, a.shape[-1])
    y_fox = _fox(r3(qf), r3(kf), r3(vf), fcum.reshape(B, N_FOX_HEADS, S // ATT_T, ATT_T))
    y_dsa = _dsa(r3(qd), r3(qi), r3(wi), r3(kd), r3(vd), r3(kk), tab)
    out = _post(x2d, y_fox.reshape(B * S, fw), y_dsa.reshape(B * S, dw), sgf, sgd, ada,
                g_norm2.reshape(1, D), g_final.reshape(1, D),
                w_branch_fox.astype(BF16), w_branch_dsa.astype(BF16), w_out.astype(BF16),
                w_mlp1.astype(BF16), w_mlp2.astype(BF16), S)
    return out.reshape(B, S, D)
```

```python
import math

import jax
import jax.numpy as jnp
from jax import lax
from jax.experimental import pallas as pl
from jax.experimental.pallas import tpu as pltpu

F32 = jnp.float32
BF16 = jnp.bfloat16
I32 = jnp.int32

HEAD_DIM = 64
N_FOX_HEADS = 8
N_DSA_HEADS = 8
N_IDX_HEADS = 4
IDX_DIM = 64
TOPK_MAX = 256
N_BUCKETS = 32
MAX_DISTANCE = 128
EPS = 1e-6

LANES = 128
SUBLANES = 8
INT_MIN = -2 ** 31
NEG = -1e30
VMEM_LIMIT = 52 * 1024 * 1024

PROJ_TM = 512
ATT_T = 512
BIAS_T = MAX_DISTANCE
POST_TM = 256
FF_CHUNK = 1024
CUM_CHUNK = 512
SMALL_ROWS = 16

_NT = (((1,), (1,)), ((), ()))


def _resident(block_shape, index_map):
    return pl.BlockSpec(block_shape, index_map, pipeline_mode=pl.Buffered(1))


def _ada_kernel(c_ref, w_ref, b_ref, o_ref):
    o_ref[...] = jnp.dot(c_ref[...], w_ref[...], preferred_element_type=F32,
                         precision=lax.Precision.HIGHEST) + b_ref[...]


def _ada(c, w_ada, b_ada):
    B, D = c.shape
    n = w_ada.shape[1]
    rows = SUBLANES
    c_pad = jnp.zeros((rows, D), F32).at[:B].set(c)
    out = pl.pallas_call(
        _ada_kernel,
        out_shape=jax.ShapeDtypeStruct((rows, n), F32),
        grid=(n // D,),
        in_specs=[pl.BlockSpec((rows, D), lambda j: (0, 0)),
                  pl.BlockSpec((D, D), lambda j: (0, j)),
                  pl.BlockSpec((1, D), lambda j: (0, j))],
        out_specs=pl.BlockSpec((rows, D), lambda j: (0, j)),
        compiler_params=pltpu.CompilerParams(vmem_limit_bytes=VMEM_LIMIT),
        name="ada",
    )(c_pad, w_ada, b_ada.reshape(1, n))
    return out[:B].reshape(B, n // D, D)


def _biastab_kernel(rb_ref, o_ref):
    h = pl.program_id(0)
    e = pl.program_id(1)
    t = o_ref.shape[-1]
    ks = lax.broadcasted_iota(I32, (t, t), 0)
    qs = lax.broadcasted_iota(I32, (t, t), 1)
    n = jnp.maximum(qs - ks + e * t, 0)
    max_exact = N_BUCKETS // 2
    nf = jnp.maximum(n, 1).astype(F32)
    large = max_exact + (jnp.log(nf / max_exact) / math.log(MAX_DISTANCE / max_exact)
                         * (N_BUCKETS - max_exact)).astype(I32)
    large = jnp.minimum(large, N_BUCKETS - 1)
    bucket = jnp.where(n < max_exact, n, large)
    acc = jnp.zeros((t, t), F32)
    for b in range(N_BUCKETS):
        acc = jnp.where(bucket == b, rb_ref[b, h], acc)
    o_ref[0, 0] = acc - rb_ref[N_BUCKETS - 1, h]


def _biastab(rel_bias, t):
    nh = rel_bias.shape[1]
    return pl.pallas_call(
        _biastab_kernel,
        out_shape=jax.ShapeDtypeStruct((nh, 2, t, t), F32),
        grid=(nh, 2),
        in_specs=[pl.BlockSpec(memory_space=pltpu.SMEM)],
        out_specs=pl.BlockSpec((1, 1, t, t), lambda h, e: (h, e, 0, 0)),
        name="biastab",
    )(rel_bias)


def _rms_mod(x, g, scale, shift):
    y = x * lax.rsqrt(jnp.mean(x * x, axis=-1, keepdims=True) + EPS) * g
    return y * (1.0 + scale) + shift


def _proj_kernel(x_ref, ada_ref, g_ref, w_ref, wt_ref, ws_ref,
                 qf_ref, kf_ref, qd_ref, kd_ref, qi_ref, kk_ref, sgf_ref, sgd_ref,
                 vft_ref, vdt_ref, small_ref):
    h = _rms_mod(x_ref[...], g_ref[...], ada_ref[0, 1:2, :], ada_ref[0, 0:1, :])
    hb = h.astype(BF16)
    col = 0
    for ref in (qf_ref, kf_ref, qd_ref, kd_ref, qi_ref, kk_ref):
        n = ref.shape[-1]
        r = jnp.dot(hb, w_ref[:, col:col + n], preferred_element_type=F32)
        ref[...] = r.astype(ref.dtype)
        col += n
    for ref in (sgf_ref, sgd_ref):
        n = ref.shape[-1]
        r = jnp.dot(hb, w_ref[:, col:col + n], preferred_element_type=F32)
        ref[...] = (1.0 / (1.0 + jnp.exp(-r))).astype(ref.dtype)
        col += n
    row = 0
    for ref in (vft_ref, vdt_ref):
        nt, n, t = ref.shape
        r = lax.dot_general(wt_ref[row:row + n, :], hb, _NT, preferred_element_type=F32)
        for u in range(nt):
            ref[u] = r[:, u * t:(u + 1) * t].astype(ref.dtype)
        row += n
    small_ref[0] = lax.dot_general(ws_ref[...], hb, _NT, preferred_element_type=F32)


def _proj(x2d, ada, g1, w_main, w_t, w_small_t, B, S):
    M, D = x2d.shape
    tm = PROJ_TM
    tpb = S // tm
    t = ATT_T
    fw = N_FOX_HEADS * HEAD_DIM
    iw = N_IDX_HEADS * IDX_DIM
    widths = [fw, fw, fw, fw, iw, LANES, D, D]
    out_shape = [jax.ShapeDtypeStruct((M, w), BF16) for w in widths]
    out_specs = [pl.BlockSpec((tm, w), lambda i: (i, 0)) for w in widths]
    for _ in range(2):
        out_shape.append(jax.ShapeDtypeStruct((M // t, fw, t), BF16))
        out_specs.append(pl.BlockSpec((tm // t, fw, t), lambda i: (i, 0, 0)))
    out_shape.append(jax.ShapeDtypeStruct((B, SMALL_ROWS, S), F32))
    out_specs.append(pl.BlockSpec((1, SMALL_ROWS, tm), lambda i: (i // tpb, 0, i % tpb)))
    return pl.pallas_call(
        _proj_kernel,
        out_shape=out_shape,
        grid=(M // tm,),
        in_specs=[pl.BlockSpec((tm, D), lambda i: (i, 0)),
                  pl.BlockSpec((1,) + ada.shape[1:], lambda i: (i // tpb, 0, 0)),
                  pl.BlockSpec((1, D), lambda i: (0, 0)),
                  _resident(w_main.shape, lambda i: (0, 0)),
                  _resident(w_t.shape, lambda i: (0, 0)),
                  _resident(w_small_t.shape, lambda i: (0, 0))],
        out_specs=out_specs,
        compiler_params=pltpu.CompilerParams(dimension_semantics=("arbitrary",),
                                             vmem_limit_bytes=VMEM_LIMIT),
        name="proj",
    )(x2d, ada, g1, w_main, w_t, w_small_t)


def _fcum_kernel(f_ref, b_ref, o_ref):
    x = f_ref[0] + b_ref[...]
    logf = -(jnp.maximum(-x, 0.0) + jnp.log1p(jnp.exp(-jnp.abs(x))))
    ch = CUM_CHUNK
    r = lax.broadcasted_iota(I32, (ch, ch), 0)
    c = lax.broadcasted_iota(I32, (ch, ch), 1)
    upper = (r <= c).astype(F32)
    carry = jnp.zeros((x.shape[0], 1), F32)
    for k in range(x.shape[1] // ch):
        seg = logf[:, k * ch:(k + 1) * ch]
        cs = jnp.dot(seg, upper, preferred_element_type=F32,
                     precision=lax.Precision.HIGHEST) + carry
        o_ref[0, :, k * ch:(k + 1) * ch] = cs
        carry = cs[:, ch - 1:ch]


def _fcum(small, b_forget):
    B, _, S = small.shape
    H = N_FOX_HEADS
    return pl.pallas_call(
        _fcum_kernel,
        out_shape=jax.ShapeDtypeStruct((B, H, S), F32),
        grid=(B,),
        in_specs=[pl.BlockSpec((1, H, S), lambda b: (b, 0, 0)),
                  pl.BlockSpec((H, 1), lambda b: (0, 0))],
        out_specs=pl.BlockSpec((1, H, S), lambda b: (b, 0, 0)),
        name="fcum",
    )(small, b_forget.reshape(H, 1))


def _flash_step(st, vt, m_ref, l_ref, acc_ref, e):
    m_prev = m_ref[e]
    m_new = jnp.maximum(m_prev, jnp.max(st, axis=0, keepdims=True))
    alpha = jnp.exp(m_prev - m_new)
    p = jnp.exp(st - m_new)
    l_ref[e] = alpha * l_ref[e] + jnp.sum(p, axis=0, keepdims=True)
    acc_ref[e] = alpha * acc_ref[e] + jnp.dot(vt, p.astype(BF16), preferred_element_type=F32)
    m_ref[e] = m_new


def _init_softmax(m_ref, l_ref, acc_ref):
    m_ref[...] = jnp.full(m_ref.shape, NEG, F32)
    l_ref[...] = jnp.zeros(l_ref.shape, F32)
    acc_ref[...] = jnp.zeros(acc_ref.shape, F32)


def _pair_output(m_ref, l_ref, acc_ref):
    a0 = acc_ref[0] / l_ref[0]
    a1 = acc_ref[1] / l_ref[1]
    out_t = jnp.concatenate([a0[0:HEAD_DIM], a1[HEAD_DIM:LANES]], axis=0)
    return out_t.T


def _split_pair(qp, lane):
    qf = qp.astype(F32)
    return (jnp.where(lane < HEAD_DIM, qf, 0.0).astype(qp.dtype),
            jnp.where(lane >= HEAD_DIM, qf, 0.0).astype(qp.dtype))


def _fox_kernel(q_ref, k_ref, vt_ref, f_ref, o_ref, acc_ref, m_ref, l_ref):
    i = pl.program_id(2)
    t = q_ref.shape[1]
    lane = lax.broadcasted_iota(I32, (t, LANES), 1)
    qm = _split_pair(q_ref[0], lane)
    _init_softmax(m_ref, l_ref, acc_ref)
    ks = lax.broadcasted_iota(I32, (t, t), 0)
    qs = lax.broadcasted_iota(I32, (t, t), 1)

    def tile(c, masked):
        off = pl.multiple_of(c * t, t)
        kt = k_ref[0, pl.ds(off, t), :]
        vt = vt_ref[c]
        for e in range(2):
            fs = f_ref[0, e, pl.ds(off, t), :]
            st = lax.dot_general(kt, qm[e], _NT, preferred_element_type=F32) - fs
            if masked:
                st = jnp.where(ks <= qs, st, NEG)
            _flash_step(st, vt, m_ref, l_ref, acc_ref, e)

    def body(c, carry):
        tile(c, False)
        return carry

    lax.fori_loop(0, i, body, 0)
    tile(i, True)
    o_ref[0] = _pair_output(m_ref, l_ref, acc_ref).astype(o_ref.dtype)


def _fox(qf, kf, vft, f4):
    B, S, W = qf.shape
    t = ATT_T
    npair = W // LANES
    nk = S // t
    return pl.pallas_call(
        _fox_kernel,
        out_shape=jax.ShapeDtypeStruct((B, S, W), BF16),
        grid=(B, npair, nk),
        in_specs=[pl.BlockSpec((1, t, LANES), lambda b, j, i: (b, i, j)),
                  pl.BlockSpec((1, S, LANES), lambda b, j, i: (b, 0, j)),
                  pl.BlockSpec((nk, LANES, t), lambda b, j, i: (b, j, 0)),
                  pl.BlockSpec((1, 2, S, 1), lambda b, j, i: (b, j, 0, 0))],
        out_specs=pl.BlockSpec((1, t, LANES), lambda b, j, i: (b, i, j)),
        scratch_shapes=[pltpu.VMEM((2, LANES, t), F32),
                        pltpu.VMEM((2, 1, t), F32),
                        pltpu.VMEM((2, 1, t), F32)],
        compiler_params=pltpu.CompilerParams(
            dimension_semantics=("arbitrary", "arbitrary", "arbitrary"),
            vmem_limit_bytes=VMEM_LIMIT),
        name="fox",
    )(qf, kf, vft, f4)


def _bias_tile(tab_ref, h, t, previous):
    nb = t // BIAS_T
    zero = jnp.zeros((BIAS_T, BIAS_T), F32)
    rows = []
    for a in range(nb):
        blocks = []
        for b in range(nb):
            d = b - a + (nb if previous else 0)
            blocks.append(tab_ref[h, d] if d in (0, 1) else zero)
        rows.append(jnp.concatenate(blocks, axis=1))
    return jnp.concatenate(rows, axis=0)


def _dsa_kernel(qd_ref, qi_ref, w_ref, kd_ref, vt_ref, kk_ref, tab_ref, o_ref,
                keys_ref, qm_ref, qim_ref, acc_ref, m_ref, l_ref):
    i = pl.program_id(1)
    t = qd_ref.shape[1]
    S = kd_ref.shape[1]
    topk = float(min(TOPK_MAX, S // 4))
    lane = lax.broadcasted_iota(I32, (t, LANES), 1)
    ks = lax.broadcasted_iota(I32, (t, t), 0)
    qs = lax.broadcasted_iota(I32, (t, t), 1)
    causal = ks <= qs

    for p in range(N_DSA_HEADS // 2):
        a, b = _split_pair(qd_ref[0, :, p * LANES:(p + 1) * LANES], lane)
        qm_ref[2 * p] = a
        qm_ref[2 * p + 1] = b
    for p in range(N_IDX_HEADS // 2):
        a, b = _split_pair(qi_ref[0, :, p * LANES:(p + 1) * LANES], lane)
        qim_ref[2 * p] = a
        qim_ref[2 * p + 1] = b
    wrows = [w_ref[0, h:h + 1, :] * (N_IDX_HEADS ** -0.5) for h in range(N_IDX_HEADS)]

    def score_chunk(c, diag):
        kt = kk_ref[0, pl.ds(pl.multiple_of(c * t, t), t), :]
        sc = None
        for h in range(N_IDX_HEADS):
            d = lax.dot_general(kt, qim_ref[h], _NT, preferred_element_type=F32)
            term = jnp.maximum(d, 0.0) * wrows[h]
            sc = term if sc is None else sc + term
        sc = jnp.where(sc == 0.0, 0.0, sc)
        bits = lax.bitcast_convert_type(sc, I32)
        key = jnp.where(bits < 0, bits ^ 0x7FFFFFFF, bits)
        if diag:
            key = jnp.where(causal, key, INT_MIN)
        keys_ref[c] = key

    def score_body(c, carry):
        score_chunk(c, False)
        return carry

    lax.fori_loop(0, i, score_body, 0)
    score_chunk(i, True)
    nch = i + 1

    def count(pred):
        def body(c, acc):
            v = jnp.where(pred(keys_ref[c]), 1.0, 0.0)
            return acc + jnp.sum(v.reshape(t // SUBLANES, SUBLANES, t), axis=0)
        acc = lax.fori_loop(0, nch, body, jnp.zeros((SUBLANES, t), F32))
        return jnp.sum(acc, axis=0, keepdims=True)

    def tau_step(jj, prefix):
        cand = prefix + jnp.left_shift(jnp.int32(1), 31 - jj)
        cnt = count(lambda x: x >= cand)
        return jnp.where(cnt >= topk, cand, prefix)

    tau = lax.fori_loop(0, 32, tau_step, jnp.full((1, t), INT_MIN, I32))
    need = topk - count(lambda x: x > tau)

    lower = jnp.where(qs < ks, 1.0, 0.0).astype(BF16)

    def mask_chunk(c, before, diag):
        x = keys_ref[c]
        eq = x == tau
        eqf = jnp.where(eq, 1.0, 0.0)
        rank = jnp.dot(lower, eqf.astype(BF16), preferred_element_type=F32) + before
        tie = jnp.where(rank < need, 0.0, NEG)
        madd = jnp.where(x > tau, 0.0, jnp.where(eq, tie, NEG))
        if diag:
            madd = jnp.where(causal, madd, NEG)
        keys_ref[c] = lax.bitcast_convert_type(madd, I32)
        return before + jnp.sum(eqf, axis=0, keepdims=True)

    before = lax.fori_loop(0, i, lambda c, b: mask_chunk(c, b, False), jnp.zeros((1, t), F32))
    mask_chunk(i, before, True)

    for p in range(N_DSA_HEADS // 2):
        _init_softmax(m_ref, l_ref, acc_ref)

        def attend(c, near):
            off = pl.multiple_of(c * t, t)
            kt = kd_ref[0, pl.ds(off, t), p * LANES:(p + 1) * LANES]
            vt = vt_ref[c, p * LANES:(p + 1) * LANES, :]
            madd = lax.bitcast_convert_type(keys_ref[c], F32)
            for e in range(2):
                h = 2 * p + e
                st = lax.dot_general(kt, qm_ref[h], _NT, preferred_element_type=F32) + madd
                if near is not None:
                    st = st + _bias_tile(tab_ref, h, t, near == 1)
                _flash_step(st, vt, m_ref, l_ref, acc_ref, e)

        def far_body(c, carry):
            attend(c, None)
            return carry

        lax.fori_loop(0, jnp.maximum(i - 1, 0), far_body, 0)

        @pl.when(i >= 1)
        def _():
            attend(i - 1, 1)

        attend(i, 0)
        o_ref[0, :, p * LANES:(p + 1) * LANES] = _pair_output(m_ref, l_ref, acc_ref).astype(o_ref.dtype)


def _dsa(qd, qi, small, kd, vdt, kk, tab):
    B, S, W = qd.shape
    t = ATT_T
    nk = S // t
    return pl.pallas_call(
        _dsa_kernel,
        out_shape=jax.ShapeDtypeStruct((B, S, W), BF16),
        grid=(B, nk),
        in_specs=[pl.BlockSpec((1, t, W), lambda b, i: (b, i, 0)),
                  pl.BlockSpec((1, t, qi.shape[2]), lambda b, i: (b, i, 0)),
                  pl.BlockSpec((1, SUBLANES, t), lambda b, i: (b, 1, i)),
                  _resident((1, S, W), lambda b, i: (b, 0, 0)),
                  _resident((nk, W, t), lambda b, i: (b, 0, 0)),
                  _resident((1, S, LANES), lambda b, i: (b, 0, 0)),
                  _resident(tab.shape, lambda b, i: (0, 0, 0, 0))],
        out_specs=pl.BlockSpec((1, t, W), lambda b, i: (b, i, 0)),
        scratch_shapes=[pltpu.VMEM((nk, t, t), I32),
                        pltpu.VMEM((N_DSA_HEADS, t, LANES), BF16),
                        pltpu.VMEM((N_IDX_HEADS, t, LANES), BF16),
                        pltpu.VMEM((2, LANES, t), F32),
                        pltpu.VMEM((2, 1, t), F32),
                        pltpu.VMEM((2, 1, t), F32)],
        compiler_params=pltpu.CompilerParams(
            dimension_semantics=("arbitrary", "arbitrary"),
            vmem_limit_bytes=VMEM_LIMIT),
        name="dsa",
    )(qd, qi, small, kd, vdt, kk, tab)


def _post_kernel(x_ref, yf_ref, yd_ref, sgf_ref, sgd_ref, ada_ref, g2_ref, gf_ref,
                 wbf_ref, wbd_ref, wo_ref, w1_ref, w2_ref, o_ref):
    bf = jnp.dot(yf_ref[...], wbf_ref[...], preferred_element_type=F32)
    bd = jnp.dot(yd_ref[...], wbd_ref[...], preferred_element_type=F32)
    merged = sgf_ref[...].astype(F32) * bf + sgd_ref[...].astype(F32) * bd
    o = jnp.dot(merged.astype(BF16), wo_ref[...], preferred_element_type=F32)
    x1 = x_ref[...] + ada_ref[0, 2:3, :] * o
    h2 = _rms_mod(x1, g2_ref[...], ada_ref[0, 4:5, :], ada_ref[0, 3:4, :]).astype(BF16)
    acc = jnp.zeros(x1.shape, F32)
    for k in range(w1_ref.shape[1] // FF_CHUNK):
        u = jnp.dot(h2, w1_ref[:, k * FF_CHUNK:(k + 1) * FF_CHUNK], preferred_element_type=F32)
        u = jnp.square(jnp.maximum(u, 0.0)).astype(BF16)
        acc = acc + jnp.dot(u, w2_ref[k * FF_CHUNK:(k + 1) * FF_CHUNK, :],
                            preferred_element_type=F32)
    x2 = x1 + ada_ref[0, 5:6, :] * acc
    y = x2 * lax.rsqrt(jnp.mean(x2 * x2, axis=-1, keepdims=True) + EPS) * gf_ref[...]
    o_ref[...] = y


def _post(x2d, yf, yd, sgf, sgd, ada, g2, gfin, wbf, wbd, wo, w1, w2, S):
    M, D = x2d.shape
    tm = POST_TM
    tpb = S // tm
    rows = lambda w: pl.BlockSpec((tm, w), lambda i: (i, 0))
    full = lambda a: _resident(a.shape, lambda i: (0,) * a.ndim)
    return pl.pallas_call(
        _post_kernel,
        out_shape=jax.ShapeDtypeStruct((M, D), F32),
        grid=(M // tm,),
        in_specs=[rows(D), rows(yf.shape[1]), rows(yd.shape[1]), rows(D), rows(D),
                  pl.BlockSpec((1,) + ada.shape[1:], lambda i: (i // tpb, 0, 0)),
                  pl.BlockSpec((1, D), lambda i: (0, 0)),
                  pl.BlockSpec((1, D), lambda i: (0, 0)),
                  full(wbf), full(wbd), full(wo), full(w1), full(w2)],
        out_specs=rows(D),
        compiler_params=pltpu.CompilerParams(dimension_semantics=("arbitrary",),
                                             vmem_limit_bytes=VMEM_LIMIT),
        name="post",
    )(x2d, yf, yd, sgf, sgd, ada, g2, gfin, wbf, wbd, wo, w1, w2)


def kernel(x, c, w_ada, b_ada, g_norm1, w_in, b_forget, rel_bias, w_branch_fox,
           w_branch_dsa, w_out, g_norm2, w_mlp1, w_mlp2, g_final):
    B, S, D = x.shape
    fw = N_FOX_HEADS * HEAD_DIM
    dw = N_DSA_HEADS * HEAD_DIM
    iw = N_IDX_HEADS * IDX_DIM
    sizes = [fw, fw, fw, N_FOX_HEADS, dw, dw, dw, iw, IDX_DIM, N_IDX_HEADS, D, D]
    offs = [0]
    for s_ in sizes:
        offs.append(offs[-1] + s_)
    seg = lambda k: w_in[:, offs[k]:offs[k + 1]]
    q_scale = HEAD_DIM ** -0.5
    i_scale = IDX_DIM ** -0.5
    w_main = jnp.concatenate(
        [seg(0) * q_scale, seg(1), seg(4) * q_scale, seg(5),
         seg(7) * i_scale, seg(8), seg(8), seg(10), seg(11)], axis=1).astype(BF16)
    w_t = jnp.concatenate([seg(2), seg(6)], axis=1).T.astype(BF16)
    w_small_t = (jnp.zeros((SMALL_ROWS, D), F32)
                 .at[:N_FOX_HEADS].set(seg(3).T)
                 .at[SUBLANES:SUBLANES + N_IDX_HEADS].set(seg(9).T)).astype(BF16)

    ada = _ada(c, w_ada, b_ada)
    tab = _biastab(rel_bias, BIAS_T)
    x2d = x.reshape(B * S, D)
    (qf, kf, qd, kd, qi, kk, sgf, sgd, vft, vdt, small) = _proj(
        x2d, ada, g_norm1.reshape(1, D), w_main, w_t, w_small_t, B, S)
    fcum = _fcum(small, b_forget)
    r3 = lambda a: a.reshape(B, S, a.shape[-1])
    y_fox = _fox(r3(qf), r3(kf), vft, fcum.reshape(B, N_FOX_HEADS, S, 1))
    y_dsa = _dsa(r3(qd), r3(qi), small, r3(kd), vdt, r3(kk), tab)
    out = _post(x2d, y_fox.reshape(B * S, fw), y_dsa.reshape(B * S, dw), sgf, sgd, ada,
                g_norm2.reshape(1, D), g_final.reshape(1, D),
                w_branch_fox.astype(BF16), w_branch_dsa.astype(BF16), w_out.astype(BF16),
                w_mlp1.astype(BF16), w_mlp2.astype(BF16), S)
    return out.reshape(B, S, D)
```

```python
import math

import jax
import jax.numpy as jnp
from jax import lax
from jax.experimental import pallas as pl
from jax.experimental.pallas import tpu as pltpu

F32 = jnp.float32
BF16 = jnp.bfloat16
I32 = jnp.int32

HEAD_DIM = 64
N_FOX_HEADS = 8
N_DSA_HEADS = 8
N_IDX_HEADS = 4
IDX_DIM = 64
TOPK_MAX = 256
N_BUCKETS = 32
MAX_DISTANCE = 128
EPS = 1e-6
LOG2E = math.log2(math.e)

LANES = 128
SUBLANES = 8
INT_MIN = -2 ** 31
NEG = -1e30
VMEM_LIMIT = 52 * 1024 * 1024

PROJ_TM = 512
ATT_T = 512
BIAS_T = MAX_DISTANCE
POST_TM = 256
FF_CHUNK = 1024
CUM_CHUNK = 512
SMALL_ROWS = 16

_NT = (((1,), (1,)), ((), ()))


def _resident(block_shape, index_map):
    return pl.BlockSpec(block_shape, index_map, pipeline_mode=pl.Buffered(1))


def _ada_kernel(c_ref, w_ref, b_ref, o_ref):
    o_ref[...] = jnp.dot(c_ref[...], w_ref[...], preferred_element_type=F32,
                         precision=lax.Precision.HIGHEST) + b_ref[...]


def _ada(c, w_ada, b_ada):
    B, D = c.shape
    n = w_ada.shape[1]
    rows = SUBLANES
    c_pad = jnp.zeros((rows, D), F32).at[:B].set(c)
    out = pl.pallas_call(
        _ada_kernel,
        out_shape=jax.ShapeDtypeStruct((rows, n), F32),
        grid=(n // D,),
        in_specs=[pl.BlockSpec((rows, D), lambda j: (0, 0)),
                  pl.BlockSpec((D, D), lambda j: (0, j)),
                  pl.BlockSpec((1, D), lambda j: (0, j))],
        out_specs=pl.BlockSpec((rows, D), lambda j: (0, j)),
        compiler_params=pltpu.CompilerParams(vmem_limit_bytes=VMEM_LIMIT),
        name="ada",
    )(c_pad, w_ada, b_ada.reshape(1, n))
    return out[:B].reshape(B, n // D, D)


def _biastab_kernel(rb_ref, o_ref):
    h = pl.program_id(0)
    e = pl.program_id(1)
    t = o_ref.shape[-1]
    ks = lax.broadcasted_iota(I32, (t, t), 0)
    qs = lax.broadcasted_iota(I32, (t, t), 1)
    n = jnp.maximum(qs - ks + e * t, 0)
    max_exact = N_BUCKETS // 2
    nf = jnp.maximum(n, 1).astype(F32)
    large = max_exact + (jnp.log(nf / max_exact) / math.log(MAX_DISTANCE / max_exact)
                         * (N_BUCKETS - max_exact)).astype(I32)
    large = jnp.minimum(large, N_BUCKETS - 1)
    bucket = jnp.where(n < max_exact, n, large)
    acc = jnp.zeros((t, t), F32)
    for b in range(N_BUCKETS):
        acc = jnp.where(bucket == b, rb_ref[b, h], acc)
    o_ref[0, 0] = (acc - rb_ref[N_BUCKETS - 1, h]) * LOG2E


def _biastab(rel_bias, t):
    nh = rel_bias.shape[1]
    return pl.pallas_call(
        _biastab_kernel,
        out_shape=jax.ShapeDtypeStruct((nh, 2, t, t), F32),
        grid=(nh, 2),
        in_specs=[pl.BlockSpec(memory_space=pltpu.SMEM)],
        out_specs=pl.BlockSpec((1, 1, t, t), lambda h, e: (h, e, 0, 0)),
        name="biastab",
    )(rel_bias)


def _rms_mod(x, g, scale, shift):
    y = x * lax.rsqrt(jnp.mean(x * x, axis=-1, keepdims=True) + EPS) * g
    return y * (1.0 + scale) + shift


def _proj_kernel(x_ref, ada_ref, g_ref, w_ref, wt_ref, ws_ref,
                 qf_ref, kf_ref, qd_ref, kd_ref, qi_ref, kk_ref, sgf_ref, sgd_ref,
                 vft_ref, vdt_ref, small_ref):
    h = _rms_mod(x_ref[...], g_ref[...], ada_ref[0, 1:2, :], ada_ref[0, 0:1, :])
    hb = h.astype(BF16)
    col = 0
    for ref in (qf_ref, kf_ref, qd_ref, kd_ref, qi_ref, kk_ref):
        n = ref.shape[-1]
        r = jnp.dot(hb, w_ref[:, col:col + n], preferred_element_type=F32)
        ref[...] = r.astype(ref.dtype)
        col += n
    for ref in (sgf_ref, sgd_ref):
        n = ref.shape[-1]
        r = jnp.dot(hb, w_ref[:, col:col + n], preferred_element_type=F32)
        ref[...] = (1.0 / (1.0 + jnp.exp(-r))).astype(ref.dtype)
        col += n
    row = 0
    for ref in (vft_ref, vdt_ref):
        nt, n, t = ref.shape
        r = lax.dot_general(wt_ref[row:row + n, :], hb, _NT, preferred_element_type=F32)
        for u in range(nt):
            ref[u] = r[:, u * t:(u + 1) * t].astype(ref.dtype)
        row += n
    small_ref[0] = lax.dot_general(ws_ref[...], hb, _NT, preferred_element_type=F32)


def _proj(x2d, ada, g1, w_main, w_t, w_small_t, B, S):
    M, D = x2d.shape
    tm = PROJ_TM
    tpb = S // tm
    t = ATT_T
    fw = N_FOX_HEADS * HEAD_DIM
    iw = N_IDX_HEADS * IDX_DIM
    widths = [fw, fw, fw, fw, iw, LANES, D, D]
    out_shape = [jax.ShapeDtypeStruct((M, w), BF16) for w in widths]
    out_specs = [pl.BlockSpec((tm, w), lambda i: (i, 0)) for w in widths]
    for _ in range(2):
        out_shape.append(jax.ShapeDtypeStruct((M // t, fw, t), BF16))
        out_specs.append(pl.BlockSpec((tm // t, fw, t), lambda i: (i, 0, 0)))
    out_shape.append(jax.ShapeDtypeStruct((B, SMALL_ROWS, S), F32))
    out_specs.append(pl.BlockSpec((1, SMALL_ROWS, tm), lambda i: (i // tpb, 0, i % tpb)))
    return pl.pallas_call(
        _proj_kernel,
        out_shape=out_shape,
        grid=(M // tm,),
        in_specs=[pl.BlockSpec((tm, D), lambda i: (i, 0)),
                  pl.BlockSpec((1,) + ada.shape[1:], lambda i: (i // tpb, 0, 0)),
                  pl.BlockSpec((1, D), lambda i: (0, 0)),
                  _resident(w_main.shape, lambda i: (0, 0)),
                  _resident(w_t.shape, lambda i: (0, 0)),
                  _resident(w_small_t.shape, lambda i: (0, 0))],
        out_specs=out_specs,
        compiler_params=pltpu.CompilerParams(dimension_semantics=("arbitrary",),
                                             vmem_limit_bytes=VMEM_LIMIT),
        name="proj",
    )(x2d, ada, g1, w_main, w_t, w_small_t)


def _fcum_kernel(f_ref, b_ref, o_ref):
    x = f_ref[0] + b_ref[...]
    logf = -(jnp.maximum(-x, 0.0) + jnp.log1p(jnp.exp(-jnp.abs(x))))
    ch = CUM_CHUNK
    r = lax.broadcasted_iota(I32, (ch, ch), 0)
    c = lax.broadcasted_iota(I32, (ch, ch), 1)
    upper = (r <= c).astype(F32)
    carry = jnp.zeros((x.shape[0], 1), F32)
    for k in range(x.shape[1] // ch):
        seg = logf[:, k * ch:(k + 1) * ch]
        cs = jnp.dot(seg, upper, preferred_element_type=F32,
                     precision=lax.Precision.HIGHEST) + carry
        o_ref[0, :, k * ch:(k + 1) * ch] = cs * LOG2E
        carry = cs[:, ch - 1:ch]


def _fcum(small, b_forget):
    B, _, S = small.shape
    H = N_FOX_HEADS
    return pl.pallas_call(
        _fcum_kernel,
        out_shape=jax.ShapeDtypeStruct((B, H, S), F32),
        grid=(B,),
        in_specs=[pl.BlockSpec((1, H, S), lambda b: (b, 0, 0)),
                  pl.BlockSpec((H, 1), lambda b: (0, 0))],
        out_specs=pl.BlockSpec((1, H, S), lambda b: (b, 0, 0)),
        name="fcum",
    )(small, b_forget.reshape(H, 1))


def _softmax_stage(s_ref, p_ref, a_ref, m_ref, l_ref, extra):
    for e in range(2):
        st = s_ref[e]
        if extra is not None:
            st = extra(e, st)
        m_prev = m_ref[e]
        m_new = jnp.maximum(m_prev, jnp.max(st, axis=0, keepdims=True))
        alpha = jnp.exp2(m_prev - m_new)
        p = jnp.exp2(st - m_new)
        l_ref[e] = alpha * l_ref[e] + jnp.sum(p, axis=0, keepdims=True)
        m_ref[e] = m_new
        p_ref[e] = p.astype(BF16)
        a_ref[e] = alpha


def _values_stage(vt, p_ref, a_ref, acc_ref):
    for e in range(2):
        acc_ref[e] = a_ref[e] * acc_ref[e] + jnp.dot(vt, p_ref[e], preferred_element_type=F32)


def _causal_attention(i, scores, softmax, values, bufs):
    s_a, s_b, p_a, p_b, a_a, a_b = bufs

    @pl.when(i == 0)
    def _():
        scores(0, s_a)
        softmax(s_a, p_a, a_a, 0)
        values(0, p_a, a_a)

    @pl.when(i >= 1)
    def _():
        p_b[...] = jnp.zeros(p_b.shape, BF16)
        a_b[...] = jnp.ones(a_b.shape, F32)
        scores(0, s_a)
        nd = (i - 1) // 2

        def double_step(jj, carry):
            c = 2 * jj
            scores(c + 1, s_b)
            softmax(s_a, p_a, a_a, None)
            values(jnp.maximum(c - 1, 0), p_b, a_b)
            scores(c + 2, s_a)
            softmax(s_b, p_b, a_b, None)
            values(c, p_a, a_a)
            return carry

        lax.fori_loop(0, nd, double_step, 0)
        c = 2 * nd
        prev = jnp.maximum(c - 1, 0)

        @pl.when(i - c == 1)
        def _():
            scores(i, s_b)
            softmax(s_a, p_a, a_a, 1)
            values(prev, p_b, a_b)
            softmax(s_b, p_b, a_b, 0)
            values(c, p_a, a_a)
            values(i, p_b, a_b)

        @pl.when(i - c == 2)
        def _():
            scores(c + 1, s_b)
            softmax(s_a, p_a, a_a, None)
            values(prev, p_b, a_b)
            scores(i, s_a)
            softmax(s_b, p_b, a_b, 1)
            values(c, p_a, a_a)
            softmax(s_a, p_a, a_a, 0)
            values(c + 1, p_b, a_b)
            values(i, p_a, a_a)


def _attention_scratch(t):
    return [pltpu.VMEM((2, t, t), F32), pltpu.VMEM((2, t, t), F32),
            pltpu.VMEM((2, t, t), BF16), pltpu.VMEM((2, t, t), BF16),
            pltpu.VMEM((2, 1, t), F32), pltpu.VMEM((2, 1, t), F32),
            pltpu.VMEM((2, LANES, t), F32),
            pltpu.VMEM((2, 1, t), F32), pltpu.VMEM((2, 1, t), F32)]


def _init_softmax(m_ref, l_ref, acc_ref):
    m_ref[...] = jnp.full(m_ref.shape, NEG, F32)
    l_ref[...] = jnp.zeros(l_ref.shape, F32)
    acc_ref[...] = jnp.zeros(acc_ref.shape, F32)


def _pair_output(l_ref, acc_ref):
    a0 = acc_ref[0] / l_ref[0]
    a1 = acc_ref[1] / l_ref[1]
    out_t = jnp.concatenate([a0[0:HEAD_DIM], a1[HEAD_DIM:LANES]], axis=0)
    return out_t.T


def _split_pair(qp, lane):
    qf = qp.astype(F32)
    return (jnp.where(lane < HEAD_DIM, qf, 0.0).astype(qp.dtype),
            jnp.where(lane >= HEAD_DIM, qf, 0.0).astype(qp.dtype))


def _fox_kernel(q_ref, k_ref, vt_ref, f_ref, o_ref, qm_ref,
                s_a, s_b, p_a, p_b, a_a, a_b, acc_ref, m_ref, l_ref):
    i = pl.program_id(2)
    t = q_ref.shape[1]
    lane = lax.broadcasted_iota(I32, (t, LANES), 1)
    qm_ref[0], qm_ref[1] = _split_pair(q_ref[0], lane)
    _init_softmax(m_ref, l_ref, acc_ref)

    def scores(c, s_ref):
        off = pl.multiple_of(c * t, t)
        kt = k_ref[0, pl.ds(off, t), :]
        for e in range(2):
            s_ref[e] = (lax.dot_general(kt, qm_ref[e], _NT, preferred_element_type=F32)
                        - f_ref[0, e, pl.ds(off, t), :])

    def causal(e, st):
        ks = lax.broadcasted_iota(I32, (t, t), 0)
        qs = lax.broadcasted_iota(I32, (t, t), 1)
        return jnp.where(ks <= qs, st, NEG)

    def softmax(s_ref, p_ref, a_ref, near):
        _softmax_stage(s_ref, p_ref, a_ref, m_ref, l_ref, causal if near == 0 else None)

    def values(c, p_ref, a_ref):
        _values_stage(vt_ref[c], p_ref, a_ref, acc_ref)

    _causal_attention(i, scores, softmax, values, (s_a, s_b, p_a, p_b, a_a, a_b))
    o_ref[0] = _pair_output(l_ref, acc_ref).astype(o_ref.dtype)


def _fox(qf, kf, vft, f4):
    B, S, W = qf.shape
    t = ATT_T
    npair = W // LANES
    nk = S // t
    return pl.pallas_call(
        _fox_kernel,
        out_shape=jax.ShapeDtypeStruct((B, S, W), BF16),
        grid=(B, npair, nk),
        in_specs=[pl.BlockSpec((1, t, LANES), lambda b, j, i: (b, i, j)),
                  pl.BlockSpec((1, S, LANES), lambda b, j, i: (b, 0, j)),
                  pl.BlockSpec((nk, LANES, t), lambda b, j, i: (b, j, 0)),
                  pl.BlockSpec((1, 2, S, 1), lambda b, j, i: (b, j, 0, 0))],
        out_specs=pl.BlockSpec((1, t, LANES), lambda b, j, i: (b, i, j)),
        scratch_shapes=[pltpu.VMEM((2, t, LANES), BF16)] + _attention_scratch(t),
        compiler_params=pltpu.CompilerParams(
            dimension_semantics=("arbitrary", "arbitrary", "arbitrary"),
            vmem_limit_bytes=VMEM_LIMIT),
        name="fox",
    )(qf, kf, vft, f4)


def _bias_tile(tab_ref, h, t, previous):
    nb = t // BIAS_T
    zero = jnp.zeros((BIAS_T, BIAS_T), F32)
    rows = []
    for a in range(nb):
        blocks = []
        for b in range(nb):
            d = b - a + (nb if previous else 0)
            blocks.append(tab_ref[h, d] if d in (0, 1) else zero)
        rows.append(jnp.concatenate(blocks, axis=1))
    return jnp.concatenate(rows, axis=0)


def _dsa_kernel(qd_ref, qi_ref, w_ref, kd_ref, vt_ref, kk_ref, tab_ref, o_ref,
                keys_ref, qm_ref, qim_ref,
                s_a, s_b, p_a, p_b, a_a, a_b, acc_ref, m_ref, l_ref):
    i = pl.program_id(1)
    t = qd_ref.shape[1]
    S = kd_ref.shape[1]
    topk = float(min(TOPK_MAX, S // 4))
    lane = lax.broadcasted_iota(I32, (t, LANES), 1)
    ks = lax.broadcasted_iota(I32, (t, t), 0)
    qs = lax.broadcasted_iota(I32, (t, t), 1)
    causal = ks <= qs

    for p in range(N_DSA_HEADS // 2):
        a, b = _split_pair(qd_ref[0, :, p * LANES:(p + 1) * LANES], lane)
        qm_ref[2 * p] = a
        qm_ref[2 * p + 1] = b
    for p in range(N_IDX_HEADS // 2):
        a, b = _split_pair(qi_ref[0, :, p * LANES:(p + 1) * LANES], lane)
        qim_ref[2 * p] = a
        qim_ref[2 * p + 1] = b
    wrows = [w_ref[0, h:h + 1, :] * (N_IDX_HEADS ** -0.5) for h in range(N_IDX_HEADS)]

    def score_chunk(c, diag):
        kt = kk_ref[0, pl.ds(pl.multiple_of(c * t, t), t), :]
        sc = None
        for h in range(N_IDX_HEADS):
            d = lax.dot_general(kt, qim_ref[h], _NT, preferred_element_type=F32)
            term = jnp.maximum(d, 0.0) * wrows[h]
            sc = term if sc is None else sc + term
        sc = jnp.where(sc == 0.0, 0.0, sc)
        bits = lax.bitcast_convert_type(sc, I32)
        key = jnp.where(bits < 0, bits ^ 0x7FFFFFFF, bits)
        if diag:
            key = jnp.where(causal, key, INT_MIN)
        keys_ref[c] = key

    def score_body(c, carry):
        score_chunk(c, False)
        return carry

    lax.fori_loop(0, i, score_body, 0)
    score_chunk(i, True)
    nch = i + 1

    def count(pred):
        def body(c, acc):
            v = jnp.where(pred(keys_ref[c]), 1.0, 0.0)
            return acc + jnp.sum(v.reshape(t // SUBLANES, SUBLANES, t), axis=0)
        acc = lax.fori_loop(0, nch, body, jnp.zeros((SUBLANES, t), F32))
        return jnp.sum(acc, axis=0, keepdims=True)

    def tau_step(jj, prefix):
        cand = prefix + jnp.left_shift(jnp.int32(1), 31 - jj)
        cnt = count(lambda x: x >= cand)
        return jnp.where(cnt >= topk, cand, prefix)

    tau = lax.fori_loop(0, 32, tau_step, jnp.full((1, t), INT_MIN, I32))
    need = topk - count(lambda x: x > tau)

    lower = jnp.where(qs < ks, 1.0, 0.0).astype(BF16)

    def mask_chunk(c, before, diag):
        x = keys_ref[c]
        eq = x == tau
        eqf = jnp.where(eq, 1.0, 0.0)
        rank = jnp.dot(lower, eqf.astype(BF16), preferred_element_type=F32) + before
        tie = jnp.where(rank < need, 0.0, NEG)
        madd = jnp.where(x > tau, 0.0, jnp.where(eq, tie, NEG))
        if diag:
            madd = jnp.where(causal, madd, NEG)
        keys_ref[c] = lax.bitcast_convert_type(madd, I32)
        return before + jnp.sum(eqf, axis=0, keepdims=True)

    before = lax.fori_loop(0, i, lambda c, b: mask_chunk(c, b, False), jnp.zeros((1, t), F32))
    mask_chunk(i, before, True)

    for p in range(N_DSA_HEADS // 2):
        _init_softmax(m_ref, l_ref, acc_ref)

        def scores(c, s_ref):
            off = pl.multiple_of(c * t, t)
            kt = kd_ref[0, pl.ds(off, t), p * LANES:(p + 1) * LANES]
            madd = lax.bitcast_convert_type(keys_ref[c], F32)
            for e in range(2):
                s_ref[e] = lax.dot_general(kt, qm_ref[2 * p + e], _NT,
                                           preferred_element_type=F32) + madd

        def softmax(s_ref, p_ref, a_ref, near):
            bias = None
            if near is not None:
                bias = lambda e, st: st + _bias_tile(tab_ref, 2 * p + e, t, near == 1)
            _softmax_stage(s_ref, p_ref, a_ref, m_ref, l_ref, bias)

        def values(c, p_ref, a_ref):
            _values_stage(vt_ref[c, p * LANES:(p + 1) * LANES, :], p_ref, a_ref, acc_ref)

        _causal_attention(i, scores, softmax, values, (s_a, s_b, p_a, p_b, a_a, a_b))
        o_ref[0, :, p * LANES:(p + 1) * LANES] = _pair_output(l_ref, acc_ref).astype(o_ref.dtype)


def _dsa(qd, qi, small, kd, vdt, kk, tab):
    B, S, W = qd.shape
    t = ATT_T
    nk = S // t
    return pl.pallas_call(
        _dsa_kernel,
        out_shape=jax.ShapeDtypeStruct((B, S, W), BF16),
        grid=(B, nk),
        in_specs=[pl.BlockSpec((1, t, W), lambda b, i: (b, i, 0)),
                  pl.BlockSpec((1, t, qi.shape[2]), lambda b, i: (b, i, 0)),
                  pl.BlockSpec((1, SUBLANES, t), lambda b, i: (b, 1, i)),
                  _resident((1, S, W), lambda b, i: (b, 0, 0)),
                  _resident((nk, W, t), lambda b, i: (b, 0, 0)),
                  _resident((1, S, LANES), lambda b, i: (b, 0, 0)),
                  _resident(tab.shape, lambda b, i: (0, 0, 0, 0))],
        out_specs=pl.BlockSpec((1, t, W), lambda b, i: (b, i, 0)),
        scratch_shapes=[pltpu.VMEM((nk, t, t), I32),
                        pltpu.VMEM((N_DSA_HEADS, t, LANES), BF16),
                        pltpu.VMEM((N_IDX_HEADS, t, LANES), BF16)] + _attention_scratch(t),
        compiler_params=pltpu.CompilerParams(
            dimension_semantics=("arbitrary", "arbitrary"),
            vmem_limit_bytes=VMEM_LIMIT),
        name="dsa",
    )(qd, qi, small, kd, vdt, kk, tab)


def _post_kernel(x_ref, yf_ref, yd_ref, sgf_ref, sgd_ref, ada_ref, g2_ref, gf_ref,
                 wbf_ref, wbd_ref, wo_ref, w1_ref, w2_ref, o_ref):
    bf = jnp.dot(yf_ref[...], wbf_ref[...], preferred_element_type=F32)
    bd = jnp.dot(yd_ref[...], wbd_ref[...], preferred_element_type=F32)
    merged = sgf_ref[...].astype(F32) * bf + sgd_ref[...].astype(F32) * bd
    o = jnp.dot(merged.astype(BF16), wo_ref[...], preferred_element_type=F32)
    x1 = x_ref[...] + ada_ref[0, 2:3, :] * o
    h2 = _rms_mod(x1, g2_ref[...], ada_ref[0, 4:5, :], ada_ref[0, 3:4, :]).astype(BF16)
    acc = jnp.zeros(x1.shape, F32)
    for k in range(w1_ref.shape[1] // FF_CHUNK):
        u = jnp.dot(h2, w1_ref[:, k * FF_CHUNK:(k + 1) * FF_CHUNK], preferred_element_type=F32)
        u = jnp.square(jnp.maximum(u, 0.0)).astype(BF16)
        acc = acc + jnp.dot(u, w2_ref[k * FF_CHUNK:(k + 1) * FF_CHUNK, :],
                            preferred_element_type=F32)
    x2 = x1 + ada_ref[0, 5:6, :] * acc
    y = x2 * lax.rsqrt(jnp.mean(x2 * x2, axis=-1, keepdims=True) + EPS) * gf_ref[...]
    o_ref[...] = y


def _post(x2d, yf, yd, sgf, sgd, ada, g2, gfin, wbf, wbd, wo, w1, w2, S):
    M, D = x2d.shape
    tm = POST_TM
    tpb = S // tm
    rows = lambda w: pl.BlockSpec((tm, w), lambda i: (i, 0))
    full = lambda a: _resident(a.shape, lambda i: (0,) * a.ndim)
    return pl.pallas_call(
        _post_kernel,
        out_shape=jax.ShapeDtypeStruct((M, D), F32),
        grid=(M // tm,),
        in_specs=[rows(D), rows(yf.shape[1]), rows(yd.shape[1]), rows(D), rows(D),
                  pl.BlockSpec((1,) + ada.shape[1:], lambda i: (i // tpb, 0, 0)),
                  pl.BlockSpec((1, D), lambda i: (0, 0)),
                  pl.BlockSpec((1, D), lambda i: (0, 0)),
                  full(wbf), full(wbd), full(wo), full(w1), full(w2)],
        out_specs=rows(D),
        compiler_params=pltpu.CompilerParams(dimension_semantics=("arbitrary",),
                                             vmem_limit_bytes=VMEM_LIMIT),
        name="post",
    )(x2d, yf, yd, sgf, sgd, ada, g2, gfin, wbf, wbd, wo, w1, w2)


def kernel(x, c, w_ada, b_ada, g_norm1, w_in, b_forget, rel_bias, w_branch_fox,
           w_branch_dsa, w_out, g_norm2, w_mlp1, w_mlp2, g_final):
    B, S, D = x.shape
    fw = N_FOX_HEADS * HEAD_DIM
    dw = N_DSA_HEADS * HEAD_DIM
    iw = N_IDX_HEADS * IDX_DIM
    sizes = [fw, fw, fw, N_FOX_HEADS, dw, dw, dw, iw, IDX_DIM, N_IDX_HEADS, D, D]
    offs = [0]
    for s_ in sizes:
        offs.append(offs[-1] + s_)
    seg = lambda k: w_in[:, offs[k]:offs[k + 1]]
    q_scale = HEAD_DIM ** -0.5 * LOG2E
    i_scale = IDX_DIM ** -0.5
    w_main = jnp.concatenate(
        [seg(0) * q_scale, seg(1), seg(4) * q_scale, seg(5),
         seg(7) * i_scale, seg(8), seg(8), seg(10), seg(11)], axis=1).astype(BF16)
    w_t = jnp.concatenate([seg(2), seg(6)], axis=1).T.astype(BF16)
    w_small_t = (jnp.zeros((SMALL_ROWS, D), F32)
                 .at[:N_FOX_HEADS].set(seg(3).T)
                 .at[SUBLANES:SUBLANES + N_IDX_HEADS].set(seg(9).T)).astype(BF16)

    ada = _ada(c, w_ada, b_ada)
    tab = _biastab(rel_bias, BIAS_T)
    x2d = x.reshape(B * S, D)
    (qf, kf, qd, kd, qi, kk, sgf, sgd, vft, vdt, small) = _proj(
        x2d, ada, g_norm1.reshape(1, D), w_main, w_t, w_small_t, B, S)
    fcum = _fcum(small, b_forget)
    r3 = lambda a: a.reshape(B, S, a.shape[-1])
    y_fox = _fox(r3(qf), r3(kf), vft, fcum.reshape(B, N_FOX_HEADS, S, 1))
    y_dsa = _dsa(r3(qd), r3(qi), small, r3(kd), vdt, r3(kk), tab)
    out = _post(x2d, y_fox.reshape(B * S, fw), y_dsa.reshape(B * S, dw), sgf, sgd, ada,
                g_norm2.reshape(1, D), g_final.reshape(1, D),
                w_branch_fox.astype(BF16), w_branch_dsa.astype(BF16), w_out.astype(BF16),
                w_mlp1.astype(BF16), w_mlp2.astype(BF16), S)
    return out.reshape(B, S, D)
```

```python
import math

import jax
import jax.numpy as jnp
from jax import lax
from jax.experimental import pallas as pl
from jax.experimental.pallas import tpu as pltpu

F32 = jnp.float32
BF16 = jnp.bfloat16
I32 = jnp.int32
I16 = jnp.int16

HEAD_DIM = 64
N_FOX_HEADS = 8
N_DSA_HEADS = 8
N_IDX_HEADS = 4
IDX_DIM = 64
TOPK_MAX = 256
N_BUCKETS = 32
MAX_DISTANCE = 128
EPS = 1e-6
LOG2E = math.log2(math.e)

LANES = 128
SUBLANES = 8
PACKED_ROWS = 16
INT_MIN = -2 ** 31
I16_MIN = -2 ** 15
NEG = -1e30
VMEM_LIMIT = 52 * 1024 * 1024

PROJ_TM = 512
ATT_T = 512
BIAS_T = MAX_DISTANCE
POST_TM = 256
FF_CHUNK = 1024
CUM_CHUNK = 512
SMALL_ROWS = 16
SOFTMAX_ROWS = 64

_NT = (((1,), (1,)), ((), ()))


def _resident(block_shape, index_map):
    return pl.BlockSpec(block_shape, index_map, pipeline_mode=pl.Buffered(1))


def _ada_kernel(c_ref, w_ref, b_ref, o_ref):
    o_ref[...] = jnp.dot(c_ref[...], w_ref[...], preferred_element_type=F32,
                         precision=lax.Precision.HIGHEST) + b_ref[...]


def _ada(c, w_ada, b_ada):
    B, D = c.shape
    n = w_ada.shape[1]
    rows = SUBLANES
    c_pad = jnp.zeros((rows, D), F32).at[:B].set(c)
    out = pl.pallas_call(
        _ada_kernel,
        out_shape=jax.ShapeDtypeStruct((rows, n), F32),
        grid=(n // D,),
        in_specs=[pl.BlockSpec((rows, D), lambda j: (0, 0)),
                  pl.BlockSpec((D, D), lambda j: (0, j)),
                  pl.BlockSpec((1, D), lambda j: (0, j))],
        out_specs=pl.BlockSpec((rows, D), lambda j: (0, j)),
        compiler_params=pltpu.CompilerParams(vmem_limit_bytes=VMEM_LIMIT),
        name="ada",
    )(c_pad, w_ada, b_ada.reshape(1, n))
    return out[:B].reshape(B, n // D, D)


def _biastab_kernel(rb_ref, o_ref):
    h = pl.program_id(0)
    e = pl.program_id(1)
    t = o_ref.shape[-1]
    ks = lax.broadcasted_iota(I32, (t, t), 0)
    qs = lax.broadcasted_iota(I32, (t, t), 1)
    n = jnp.maximum(qs - ks + e * t, 0)
    max_exact = N_BUCKETS // 2
    nf = jnp.maximum(n, 1).astype(F32)
    large = max_exact + (jnp.log(nf / max_exact) / math.log(MAX_DISTANCE / max_exact)
                         * (N_BUCKETS - max_exact)).astype(I32)
    large = jnp.minimum(large, N_BUCKETS - 1)
    bucket = jnp.where(n < max_exact, n, large)
    acc = jnp.zeros((t, t), F32)
    for b in range(N_BUCKETS):
        acc = jnp.where(bucket == b, rb_ref[b, h], acc)
    o_ref[0, 0] = (acc - rb_ref[N_BUCKETS - 1, h]) * LOG2E


def _biastab(rel_bias, t):
    nh = rel_bias.shape[1]
    return pl.pallas_call(
        _biastab_kernel,
        out_shape=jax.ShapeDtypeStruct((nh, 2, t, t), F32),
        grid=(nh, 2),
        in_specs=[pl.BlockSpec(memory_space=pltpu.SMEM)],
        out_specs=pl.BlockSpec((1, 1, t, t), lambda h, e: (h, e, 0, 0)),
        name="biastab",
    )(rel_bias)


def _rms_mod(x, g, scale, shift):
    y = x * lax.rsqrt(jnp.mean(x * x, axis=-1, keepdims=True) + EPS) * g
    return y * (1.0 + scale) + shift


def _proj_kernel(x_ref, ada_ref, g_ref, w_ref, wt_ref, ws_ref,
                 qf_ref, kf_ref, qd_ref, kd_ref, qi_ref, kk_ref, sgf_ref, sgd_ref,
                 vft_ref, vdt_ref, small_ref):
    h = _rms_mod(x_ref[...], g_ref[...], ada_ref[0, 1:2, :], ada_ref[0, 0:1, :])
    hb = h.astype(BF16)
    col = 0
    for ref in (qf_ref, kf_ref, qd_ref, kd_ref, qi_ref, kk_ref):
        n = ref.shape[-1]
        r = jnp.dot(hb, w_ref[:, col:col + n], preferred_element_type=F32)
        ref[...] = r.astype(ref.dtype)
        col += n
    for ref in (sgf_ref, sgd_ref):
        n = ref.shape[-1]
        r = jnp.dot(hb, w_ref[:, col:col + n], preferred_element_type=F32)
        ref[...] = (1.0 / (1.0 + jnp.exp(-r))).astype(ref.dtype)
        col += n
    row = 0
    for ref in (vft_ref, vdt_ref):
        nt, n, t = ref.shape
        r = lax.dot_general(wt_ref[row:row + n, :], hb, _NT, preferred_element_type=F32)
        for u in range(nt):
            ref[u] = r[:, u * t:(u + 1) * t].astype(ref.dtype)
        row += n
    small_ref[0] = lax.dot_general(ws_ref[...], hb, _NT, preferred_element_type=F32)


def _proj(x2d, ada, g1, w_main, w_t, w_small_t, B, S):
    M, D = x2d.shape
    tm = PROJ_TM
    tpb = S // tm
    t = ATT_T
    fw = N_FOX_HEADS * HEAD_DIM
    iw = N_IDX_HEADS * IDX_DIM
    widths = [fw, fw, fw, fw, iw, LANES, D, D]
    out_shape = [jax.ShapeDtypeStruct((M, w), BF16) for w in widths]
    out_specs = [pl.BlockSpec((tm, w), lambda i: (i, 0)) for w in widths]
    for _ in range(2):
        out_shape.append(jax.ShapeDtypeStruct((M // t, fw, t), BF16))
        out_specs.append(pl.BlockSpec((tm // t, fw, t), lambda i: (i, 0, 0)))
    out_shape.append(jax.ShapeDtypeStruct((B, SMALL_ROWS, S), F32))
    out_specs.append(pl.BlockSpec((1, SMALL_ROWS, tm), lambda i: (i // tpb, 0, i % tpb)))
    return pl.pallas_call(
        _proj_kernel,
        out_shape=out_shape,
        grid=(M // tm,),
        in_specs=[pl.BlockSpec((tm, D), lambda i: (i, 0)),
                  pl.BlockSpec((1,) + ada.shape[1:], lambda i: (i // tpb, 0, 0)),
                  pl.BlockSpec((1, D), lambda i: (0, 0)),
                  _resident(w_main.shape, lambda i: (0, 0)),
                  _resident(w_t.shape, lambda i: (0, 0)),
                  _resident(w_small_t.shape, lambda i: (0, 0))],
        out_specs=out_specs,
        compiler_params=pltpu.CompilerParams(dimension_semantics=("arbitrary",),
                                             vmem_limit_bytes=VMEM_LIMIT),
        name="proj",
    )(x2d, ada, g1, w_main, w_t, w_small_t)


def _fcum_kernel(f_ref, b_ref, o_ref):
    x = f_ref[0] + b_ref[...]
    logf = -(jnp.maximum(-x, 0.0) + jnp.log1p(jnp.exp(-jnp.abs(x))))
    ch = CUM_CHUNK
    r = lax.broadcasted_iota(I32, (ch, ch), 0)
    c = lax.broadcasted_iota(I32, (ch, ch), 1)
    upper = (r <= c).astype(F32)
    carry = jnp.zeros((x.shape[0], 1), F32)
    for k in range(x.shape[1] // ch):
        seg = logf[:, k * ch:(k + 1) * ch]
        cs = jnp.dot(seg, upper, preferred_element_type=F32,
                     precision=lax.Precision.HIGHEST) + carry
        o_ref[0, :, k * ch:(k + 1) * ch] = cs * LOG2E
        carry = cs[:, ch - 1:ch]


def _fcum(small, b_forget):
    B, _, S = small.shape
    H = N_FOX_HEADS
    return pl.pallas_call(
        _fcum_kernel,
        out_shape=jax.ShapeDtypeStruct((B, H, S), F32),
        grid=(B,),
        in_specs=[pl.BlockSpec((1, H, S), lambda b: (b, 0, 0)),
                  pl.BlockSpec((H, 1), lambda b: (0, 0))],
        out_specs=pl.BlockSpec((1, H, S), lambda b: (b, 0, 0)),
        name="fcum",
    )(small, b_forget.reshape(H, 1))


def _softmax_stage(s_ref, p_ref, a_ref, m_ref, l_ref, extra):
    t = s_ref.shape[1]
    rb = SOFTMAX_ROWS
    fold = lambda x, op: op(x.reshape(rb // SUBLANES, SUBLANES, t), axis=0)
    for e in range(2):
        if extra is not None:
            for r in range(0, t, rb):
                s_ref[e, r:r + rb, :] = extra(e, r, s_ref[e, r:r + rb, :])
        mx = None
        for r in range(0, t, rb):
            part = fold(s_ref[e, r:r + rb, :], jnp.max)
            mx = part if mx is None else jnp.maximum(mx, part)
        m_prev = m_ref[e]
        m_new = jnp.maximum(m_prev, jnp.max(mx, axis=0, keepdims=True))
        alpha = jnp.exp2(m_prev - m_new)
        total = None
        for r in range(0, t, rb):
            p = jnp.exp2(s_ref[e, r:r + rb, :] - m_new)
            part = fold(p, jnp.sum)
            total = part if total is None else total + part
            p_ref[e, r:r + rb, :] = p.astype(BF16)
        l_ref[e] = alpha * l_ref[e] + jnp.sum(total, axis=0, keepdims=True)
        m_ref[e] = m_new
        a_ref[e] = alpha


def _values_stage(vt, p_ref, a_ref, acc_ref):
    for e in range(2):
        ve = vt[e * HEAD_DIM:(e + 1) * HEAD_DIM]
        acc_ref[e] = a_ref[e] * acc_ref[e] + jnp.dot(ve, p_ref[e], preferred_element_type=F32)


def _causal_attention(i, scores, softmax, values, bufs):
    s_a, s_b, p_a, p_b, a_a, a_b = bufs

    @pl.when(i == 0)
    def _():
        scores(0, s_a)
        softmax(s_a, p_a, a_a, 0)
        values(0, p_a, a_a)

    @pl.when(i >= 1)
    def _():
        p_b[...] = jnp.zeros(p_b.shape, BF16)
        a_b[...] = jnp.ones(a_b.shape, F32)
        scores(0, s_a)
        nd = (i - 1) // 2

        def double_step(jj, carry):
            c = 2 * jj
            scores(c + 1, s_b)
            softmax(s_a, p_a, a_a, None)
            values(jnp.maximum(c - 1, 0), p_b, a_b)
            scores(c + 2, s_a)
            softmax(s_b, p_b, a_b, None)
            values(c, p_a, a_a)
            return carry

        lax.fori_loop(0, nd, double_step, 0)
        c = 2 * nd
        prev = jnp.maximum(c - 1, 0)

        @pl.when(i - c == 1)
        def _():
            scores(i, s_b)
            softmax(s_a, p_a, a_a, 1)
            values(prev, p_b, a_b)
            softmax(s_b, p_b, a_b, 0)
            values(c, p_a, a_a)
            values(i, p_b, a_b)

        @pl.when(i - c == 2)
        def _():
            scores(c + 1, s_b)
            softmax(s_a, p_a, a_a, None)
            values(prev, p_b, a_b)
            scores(i, s_a)
            softmax(s_b, p_b, a_b, 1)
            values(c, p_a, a_a)
            softmax(s_a, p_a, a_a, 0)
            values(c + 1, p_b, a_b)
            values(i, p_a, a_a)


def _attention_scratch(t):
    return [pltpu.VMEM((2, t, t), F32), pltpu.VMEM((2, t, t), F32),
            pltpu.VMEM((2, t, t), BF16), pltpu.VMEM((2, t, t), BF16),
            pltpu.VMEM((2, 1, t), F32), pltpu.VMEM((2, 1, t), F32),
            pltpu.VMEM((2, HEAD_DIM, t), F32),
            pltpu.VMEM((2, 1, t), F32), pltpu.VMEM((2, 1, t), F32)]


def _init_softmax(m_ref, l_ref, acc_ref):
    m_ref[...] = jnp.full(m_ref.shape, NEG, F32)
    l_ref[...] = jnp.zeros(l_ref.shape, F32)
    acc_ref[...] = jnp.zeros(acc_ref.shape, F32)


def _pair_output(l_ref, acc_ref):
    out_t = jnp.concatenate([acc_ref[0] / l_ref[0], acc_ref[1] / l_ref[1]], axis=0)
    return out_t.T


def _split_pair(qp, lane):
    qf = qp.astype(F32)
    return (jnp.where(lane < HEAD_DIM, qf, 0.0).astype(qp.dtype),
            jnp.where(lane >= HEAD_DIM, qf, 0.0).astype(qp.dtype))


def _fox_kernel(q_ref, k_ref, vt_ref, f_ref, o_ref, qm_ref,
                s_a, s_b, p_a, p_b, a_a, a_b, acc_ref, m_ref, l_ref):
    i = pl.program_id(2)
    t = q_ref.shape[1]
    lane = lax.broadcasted_iota(I32, (t, LANES), 1)
    qm_ref[0], qm_ref[1] = _split_pair(q_ref[0], lane)
    _init_softmax(m_ref, l_ref, acc_ref)

    def scores(c, s_ref):
        off = pl.multiple_of(c * t, t)
        kt = k_ref[0, pl.ds(off, t), :]
        for e in range(2):
            s_ref[e] = (lax.dot_general(kt, qm_ref[e], _NT, preferred_element_type=F32)
                        - f_ref[0, e, pl.ds(off, t), :])

    def causal(e, r, block):
        ks = lax.broadcasted_iota(I32, block.shape, 0) + r
        qs = lax.broadcasted_iota(I32, block.shape, 1)
        return jnp.where(ks <= qs, block, NEG)

    def softmax(s_ref, p_ref, a_ref, near):
        _softmax_stage(s_ref, p_ref, a_ref, m_ref, l_ref, causal if near == 0 else None)

    def values(c, p_ref, a_ref):
        _values_stage(vt_ref[c], p_ref, a_ref, acc_ref)

    _causal_attention(i, scores, softmax, values, (s_a, s_b, p_a, p_b, a_a, a_b))
    o_ref[0] = _pair_output(l_ref, acc_ref).astype(o_ref.dtype)


def _fox(qf, kf, vft, f4):
    B, S, W = qf.shape
    t = ATT_T
    npair = W // LANES
    nk = S // t
    return pl.pallas_call(
        _fox_kernel,
        out_shape=jax.ShapeDtypeStruct((B, S, W), BF16),
        grid=(B, npair, nk),
        in_specs=[pl.BlockSpec((1, t, LANES), lambda b, j, i: (b, i, j)),
                  pl.BlockSpec((1, S, LANES), lambda b, j, i: (b, 0, j)),
                  pl.BlockSpec((nk, LANES, t), lambda b, j, i: (b, j, 0)),
                  pl.BlockSpec((1, 2, S, 1), lambda b, j, i: (b, j, 0, 0))],
        out_specs=pl.BlockSpec((1, t, LANES), lambda b, j, i: (b, i, j)),
        scratch_shapes=[pltpu.VMEM((2, t, LANES), BF16)] + _attention_scratch(t),
        compiler_params=pltpu.CompilerParams(
            dimension_semantics=("arbitrary", "arbitrary", "arbitrary"),
            vmem_limit_bytes=VMEM_LIMIT),
        name="fox",
    )(qf, kf, vft, f4)


def _bias_rows(tab_ref, h, t, previous, r, rows):
    nb = t // BIAS_T
    a, r0 = divmod(r, BIAS_T)
    zero = jnp.zeros((rows, BIAS_T), F32)
    blocks = []
    for b in range(nb):
        d = b - a + (nb if previous else 0)
        blocks.append(tab_ref[h, d, r0:r0 + rows, :] if d in (0, 1) else zero)
    return jnp.concatenate(blocks, axis=1)


def _dsa_kernel(qd_ref, qi_ref, w_ref, kd_ref, vt_ref, kk_ref, tab_ref, o_ref,
                keys_ref, half_ref, qm_ref, qim_ref,
                s_a, s_b, p_a, p_b, a_a, a_b, acc_ref, m_ref, l_ref):
    i = pl.program_id(1)
    t = qd_ref.shape[1]
    S = kd_ref.shape[1]
    topk = float(min(TOPK_MAX, S // 4))
    lane = lax.broadcasted_iota(I32, (t, LANES), 1)
    ks = lax.broadcasted_iota(I32, (t, t), 0)
    qs = lax.broadcasted_iota(I32, (t, t), 1)
    causal = ks <= qs

    for p in range(N_DSA_HEADS // 2):
        a, b = _split_pair(qd_ref[0, :, p * LANES:(p + 1) * LANES], lane)
        qm_ref[2 * p] = a
        qm_ref[2 * p + 1] = b
    for p in range(N_IDX_HEADS // 2):
        a, b = _split_pair(qi_ref[0, :, p * LANES:(p + 1) * LANES], lane)
        qim_ref[2 * p] = a
        qim_ref[2 * p + 1] = b
    wrows = [w_ref[0, h:h + 1, :] * (N_IDX_HEADS ** -0.5) for h in range(N_IDX_HEADS)]

    def score_chunk(c, diag):
        kt = kk_ref[0, pl.ds(pl.multiple_of(c * t, t), t), :]
        sc = None
        for h in range(N_IDX_HEADS):
            d = lax.dot_general(kt, qim_ref[h], _NT, preferred_element_type=F32)
            term = jnp.maximum(d, 0.0) * wrows[h]
            sc = term if sc is None else sc + term
        sc = jnp.where(sc == 0.0, 0.0, sc)
        bits = lax.bitcast_convert_type(sc, I32)
        key = jnp.where(bits < 0, bits ^ 0x7FFFFFFF, bits)
        if diag:
            key = jnp.where(causal, key, INT_MIN)
        keys_ref[c] = key
        half_ref[c] = jnp.right_shift(key, 16).astype(I16)

    def score_body(c, carry):
        score_chunk(c, False)
        return carry

    lax.fori_loop(0, i, score_body, 0)
    score_chunk(i, True)
    nch = i + 1

    def count(pred):
        def body(c, acc):
            v = jnp.where(pred(keys_ref[c]), 1.0, 0.0)
            return acc + jnp.sum(v.reshape(t // SUBLANES, SUBLANES, t), axis=0)
        acc = lax.fori_loop(0, nch, body, jnp.zeros((SUBLANES, t), F32))
        return jnp.sum(acc, axis=0, keepdims=True)

    def kth_largest_half():
        def count_ge(cand):
            cand_rows = jnp.broadcast_to(cand, (PACKED_ROWS, t))

            def body(c, accs):
                accs = list(accs)
                for k in range(t // PACKED_ROWS):
                    x = half_ref[c, k * PACKED_ROWS:(k + 1) * PACKED_ROWS, :]
                    hit = jnp.where(x >= cand_rows, jnp.int16(1), jnp.int16(0))
                    accs[k % len(accs)] = accs[k % len(accs)] + hit
                return tuple(accs)
            zero = jnp.zeros((PACKED_ROWS, t), I16)
            accs = lax.fori_loop(0, nch, body, (zero,) * 4)
            total = sum(a.astype(I32) for a in accs)
            return jnp.sum(total, axis=0, keepdims=True)

        def step(jj, prefix):
            cand = prefix + jnp.left_shift(jnp.int32(1), 15 - jj)
            return jnp.where(count_ge(cand.astype(I16)) >= int(topk), cand, prefix)

        return lax.fori_loop(0, 16, step, jnp.full((1, t), I16_MIN, I32))

    tau_hi = kth_largest_half()
    lo_min = jnp.left_shift(tau_hi, 16)
    lo_max = lo_min + 0xFFFF

    def low_halves(c, carry):
        x = keys_ref[c]
        low = jnp.bitwise_and(x, 0xFFFF) + I16_MIN
        low = jnp.where(x > lo_max, -I16_MIN - 1, jnp.where(x < lo_min, I16_MIN, low))
        half_ref[c] = low.astype(I16)
        return carry

    lax.fori_loop(0, nch, low_halves, 0)
    tau = lo_min + (kth_largest_half() - I16_MIN)
    need = topk - count(lambda x: x > tau)

    lower = jnp.where(qs < ks, 1.0, 0.0).astype(BF16)

    def mask_chunk(c, before, diag):
        x = keys_ref[c]
        eq = x == tau
        eqf = jnp.where(eq, 1.0, 0.0)
        rank = jnp.dot(lower, eqf.astype(BF16), preferred_element_type=F32) + before
        tie = jnp.where(rank < need, 0.0, NEG)
        madd = jnp.where(x > tau, 0.0, jnp.where(eq, tie, NEG))
        if diag:
            madd = jnp.where(causal, madd, NEG)
        keys_ref[c] = lax.bitcast_convert_type(madd, I32)
        return before + jnp.sum(eqf, axis=0, keepdims=True)

    before = lax.fori_loop(0, i, lambda c, b: mask_chunk(c, b, False), jnp.zeros((1, t), F32))
    mask_chunk(i, before, True)

    for p in range(N_DSA_HEADS // 2):
        _init_softmax(m_ref, l_ref, acc_ref)

        def scores(c, s_ref):
            off = pl.multiple_of(c * t, t)
            kt = kd_ref[0, pl.ds(off, t), p * LANES:(p + 1) * LANES]
            madd = lax.bitcast_convert_type(keys_ref[c], F32)
            for e in range(2):
                s_ref[e] = lax.dot_general(kt, qm_ref[2 * p + e], _NT,
                                           preferred_element_type=F32) + madd

        def softmax(s_ref, p_ref, a_ref, near):
            bias = None
            if near is not None:
                bias = lambda e, r, block: block + _bias_rows(
                    tab_ref, 2 * p + e, t, near == 1, r, block.shape[0])
            _softmax_stage(s_ref, p_ref, a_ref, m_ref, l_ref, bias)

        def values(c, p_ref, a_ref):
            _values_stage(vt_ref[c, p * LANES:(p + 1) * LANES, :], p_ref, a_ref, acc_ref)

        _causal_attention(i, scores, softmax, values, (s_a, s_b, p_a, p_b, a_a, a_b))
        o_ref[0, :, p * LANES:(p + 1) * LANES] = _pair_output(l_ref, acc_ref).astype(o_ref.dtype)


def _dsa(qd, qi, small, kd, vdt, kk, tab):
    B, S, W = qd.shape
    t = ATT_T
    nk = S // t
    return pl.pallas_call(
        _dsa_kernel,
        out_shape=jax.ShapeDtypeStruct((B, S, W), BF16),
        grid=(B, nk),
        in_specs=[pl.BlockSpec((1, t, W), lambda b, i: (b, i, 0)),
                  pl.BlockSpec((1, t, qi.shape[2]), lambda b, i: (b, i, 0)),
                  pl.BlockSpec((1, SUBLANES, t), lambda b, i: (b, 1, i)),
                  _resident((1, S, W), lambda b, i: (b, 0, 0)),
                  _resident((nk, W, t), lambda b, i: (b, 0, 0)),
                  _resident((1, S, LANES), lambda b, i: (b, 0, 0)),
                  _resident(tab.shape, lambda b, i: (0, 0, 0, 0))],
        out_specs=pl.BlockSpec((1, t, W), lambda b, i: (b, i, 0)),
        scratch_shapes=[pltpu.VMEM((nk, t, t), I32),
                        pltpu.VMEM((nk, t, t), I16),
                        pltpu.VMEM((N_DSA_HEADS, t, LANES), BF16),
                        pltpu.VMEM((N_IDX_HEADS, t, LANES), BF16)] + _attention_scratch(t),
        compiler_params=pltpu.CompilerParams(
            dimension_semantics=("arbitrary", "arbitrary"),
            vmem_limit_bytes=VMEM_LIMIT),
        name="dsa",
    )(qd, qi, small, kd, vdt, kk, tab)


def _post_kernel(x_ref, yf_ref, yd_ref, sgf_ref, sgd_ref, ada_ref, g2_ref, gf_ref,
                 wbf_ref, wbd_ref, wo_ref, w1_ref, w2_ref, o_ref):
    bf = jnp.dot(yf_ref[...], wbf_ref[...], preferred_element_type=F32)
    bd = jnp.dot(yd_ref[...], wbd_ref[...], preferred_element_type=F32)
    merged = sgf_ref[...].astype(F32) * bf + sgd_ref[...].astype(F32) * bd
    o = jnp.dot(merged.astype(BF16), wo_ref[...], preferred_element_type=F32)
    x1 = x_ref[...] + ada_ref[0, 2:3, :] * o
    h2 = _rms_mod(x1, g2_ref[...], ada_ref[0, 4:5, :], ada_ref[0, 3:4, :]).astype(BF16)
    acc = jnp.zeros(x1.shape, F32)
    for k in range(w1_ref.shape[1] // FF_CHUNK):
        u = jnp.dot(h2, w1_ref[:, k * FF_CHUNK:(k + 1) * FF_CHUNK], preferred_element_type=F32)
        u = jnp.square(jnp.maximum(u, 0.0)).astype(BF16)
        acc = acc + jnp.dot(u, w2_ref[k * FF_CHUNK:(k + 1) * FF_CHUNK, :],
                            preferred_element_type=F32)
    x2 = x1 + ada_ref[0, 5:6, :] * acc
    y = x2 * lax.rsqrt(jnp.mean(x2 * x2, axis=-1, keepdims=True) + EPS) * gf_ref[...]
    o_ref[...] = y


def _post(x2d, yf, yd, sgf, sgd, ada, g2, gfin, wbf, wbd, wo, w1, w2, S):
    M, D = x2d.shape
    tm = POST_TM
    tpb = S // tm
    rows = lambda w: pl.BlockSpec((tm, w), lambda i: (i, 0))
    full = lambda a: _resident(a.shape, lambda i: (0,) * a.ndim)
    return pl.pallas_call(
        _post_kernel,
        out_shape=jax.ShapeDtypeStruct((M, D), F32),
        grid=(M // tm,),
        in_specs=[rows(D), rows(yf.shape[1]), rows(yd.shape[1]), rows(D), rows(D),
                  pl.BlockSpec((1,) + ada.shape[1:], lambda i: (i // tpb, 0, 0)),
                  pl.BlockSpec((1, D), lambda i: (0, 0)),
                  pl.BlockSpec((1, D), lambda i: (0, 0)),
                  full(wbf), full(wbd), full(wo), full(w1), full(w2)],
        out_specs=rows(D),
        compiler_params=pltpu.CompilerParams(dimension_semantics=("arbitrary",),
                                             vmem_limit_bytes=VMEM_LIMIT),
        name="post",
    )(x2d, yf, yd, sgf, sgd, ada, g2, gfin, wbf, wbd, wo, w1, w2)


def kernel(x, c, w_ada, b_ada, g_norm1, w_in, b_forget, rel_bias, w_branch_fox,
           w_branch_dsa, w_out, g_norm2, w_mlp1, w_mlp2, g_final):
    B, S, D = x.shape
    fw = N_FOX_HEADS * HEAD_DIM
    dw = N_DSA_HEADS * HEAD_DIM
    iw = N_IDX_HEADS * IDX_DIM
    sizes = [fw, fw, fw, N_FOX_HEADS, dw, dw, dw, iw, IDX_DIM, N_IDX_HEADS, D, D]
    offs = [0]
    for s_ in sizes:
        offs.append(offs[-1] + s_)
    seg = lambda k: w_in[:, offs[k]:offs[k + 1]]
    q_scale = HEAD_DIM ** -0.5 * LOG2E
    i_scale = IDX_DIM ** -0.5
    w_main = jnp.concatenate(
        [seg(0) * q_scale, seg(1), seg(4) * q_scale, seg(5),
         seg(7) * i_scale, seg(8), seg(8), seg(10), seg(11)], axis=1).astype(BF16)
    w_t = jnp.concatenate([seg(2), seg(6)], axis=1).T.astype(BF16)
    w_small_t = (jnp.zeros((SMALL_ROWS, D), F32)
                 .at[:N_FOX_HEADS].set(seg(3).T)
                 .at[SUBLANES:SUBLANES + N_IDX_HEADS].set(seg(9).T)).astype(BF16)

    ada = _ada(c, w_ada, b_ada)
    tab = _biastab(rel_bias, BIAS_T)
    x2d = x.reshape(B * S, D)
    (qf, kf, qd, kd, qi, kk, sgf, sgd, vft, vdt, small) = _proj(
        x2d, ada, g_norm1.reshape(1, D), w_main, w_t, w_small_t, B, S)
    fcum = _fcum(small, b_forget)
    r3 = lambda a: a.reshape(B, S, a.shape[-1])
    y_fox = _fox(r3(qf), r3(kf), vft, fcum.reshape(B, N_FOX_HEADS, S, 1))
    y_dsa = _dsa(r3(qd), r3(qi), small, r3(kd), vdt, r3(kk), tab)
    out = _post(x2d, y_fox.reshape(B * S, fw), y_dsa.reshape(B * S, dw), sgf, sgd, ada,
                g_norm2.reshape(1, D), g_final.reshape(1, D),
                w_branch_fox.astype(BF16), w_branch_dsa.astype(BF16), w_out.astype(BF16),
                w_mlp1.astype(BF16), w_mlp2.astype(BF16), S)
    return out.reshape(B, S, D)
```

```python
import math

import jax
import jax.numpy as jnp
import numpy as np
from jax import lax
from jax.experimental import pallas as pl
from jax.experimental.pallas import tpu as pltpu

F32 = jnp.float32
BF16 = jnp.bfloat16
I32 = jnp.int32

HEAD_DIM = 64
N_FOX_HEADS = 8
N_DSA_HEADS = 8
N_IDX_HEADS = 4
IDX_DIM = 64
TOPK_MAX = 256
N_BUCKETS = 32
MAX_DISTANCE = 128
EPS = 1e-6
LOG2E = float(np.float32(math.log2(math.e)))

LANES = 128
SUBLANES = 8
INT_MIN = -2 ** 31
F32_MAX = float(np.finfo(np.float32).max)
NEG = -1e30
VMEM_LIMIT = 52 * 1024 * 1024

PROJ_TM = 512
ATT_T = 512
BIAS_T = MAX_DISTANCE
POST_TM = 256
FF_CHUNK = 1024
CUM_CHUNK = 512
SMALL_ROWS = 16
SOFTMAX_ROWS = 64
F_TERMS = 3
GAP_STEPS = 30

_NT = (((1,), (1,)), ((), ()))


def _resident(block_shape, index_map):
    return pl.BlockSpec(block_shape, index_map, pipeline_mode=pl.Buffered(1))


def _ada_kernel(c_ref, w_ref, b_ref, o_ref):
    o_ref[...] = jnp.dot(c_ref[...], w_ref[...], preferred_element_type=F32,
                         precision=lax.Precision.HIGHEST) + b_ref[...]


def _ada(c, w_ada, b_ada):
    B, D = c.shape
    n = w_ada.shape[1]
    rows = SUBLANES
    c_pad = jnp.zeros((rows, D), F32).at[:B].set(c)
    out = pl.pallas_call(
        _ada_kernel,
        out_shape=jax.ShapeDtypeStruct((rows, n), F32),
        grid=(n // D,),
        in_specs=[pl.BlockSpec((rows, D), lambda j: (0, 0)),
                  pl.BlockSpec((D, D), lambda j: (0, j)),
                  pl.BlockSpec((1, D), lambda j: (0, j))],
        out_specs=pl.BlockSpec((rows, D), lambda j: (0, j)),
        compiler_params=pltpu.CompilerParams(vmem_limit_bytes=VMEM_LIMIT),
        name="ada",
    )(c_pad, w_ada, b_ada.reshape(1, n))
    return out[:B].reshape(B, n // D, D)


def _bucket_tiles(t):
    ks = np.arange(t, dtype=np.int32)[None, :, None]
    qs = np.arange(t, dtype=np.int32)[None, None, :]
    n = np.maximum(qs - ks + np.arange(2, dtype=np.int32)[:, None, None] * t, 0)
    max_exact = N_BUCKETS // 2
    nf = np.maximum(n, 1).astype(np.float32)
    scaled = (np.log(nf / np.float32(max_exact))
              / np.float32(math.log(MAX_DISTANCE / max_exact))
              * np.float32(N_BUCKETS - max_exact)).astype(np.float32)
    large = np.minimum(max_exact + scaled.astype(np.int32), N_BUCKETS - 1)
    return np.where(n < max_exact, n, large).astype(np.int32)


def _biastab_kernel(rb_ref, bucket_ref, o_ref):
    h = pl.program_id(0)
    bucket = bucket_ref[0]
    acc = jnp.zeros(bucket.shape, F32)
    for b in range(N_BUCKETS):
        acc = jnp.where(bucket == b, rb_ref[b, h], acc)
    o_ref[0, 0] = (acc - rb_ref[N_BUCKETS - 1, h]) * LOG2E


def _biastab(rel_bias, t):
    nh = rel_bias.shape[1]
    return pl.pallas_call(
        _biastab_kernel,
        out_shape=jax.ShapeDtypeStruct((nh, 2, t, t), F32),
        grid=(nh, 2),
        in_specs=[pl.BlockSpec(memory_space=pltpu.SMEM),
                  pl.BlockSpec((1, t, t), lambda h, e: (e, 0, 0))],
        out_specs=pl.BlockSpec((1, 1, t, t), lambda h, e: (h, e, 0, 0)),
        name="biastab",
    )(rel_bias, jnp.asarray(_bucket_tiles(t)))


def _rms_mod(x, g, scale, shift):
    y = x * lax.rsqrt(jnp.mean(x * x, axis=-1, keepdims=True) + EPS) * g
    return y * (1.0 + scale) + shift


def _proj_kernel(x_ref, ada_ref, g_ref, w_ref, wt_ref, ws_ref,
                 qf_ref, kf_ref, qd_ref, kd_ref, qi_ref, kk_ref, sgf_ref, sgd_ref,
                 vft_ref, vdt_ref, small_ref):
    h = _rms_mod(x_ref[...], g_ref[...], ada_ref[0, 1:2, :], ada_ref[0, 0:1, :])
    hb = h.astype(BF16)
    col = 0
    for ref in (qf_ref, kf_ref, qd_ref, kd_ref, qi_ref, kk_ref):
        n = ref.shape[-1]
        r = jnp.dot(hb, w_ref[:, col:col + n], preferred_element_type=F32)
        ref[...] = r.astype(ref.dtype)
        col += n
    for ref in (sgf_ref, sgd_ref):
        n = ref.shape[-1]
        r = jnp.dot(hb, w_ref[:, col:col + n], preferred_element_type=F32)
        ref[...] = (1.0 / (1.0 + jnp.exp(-r))).astype(ref.dtype)
        col += n
    row = 0
    for ref in (vft_ref, vdt_ref):
        nt, n, t = ref.shape
        r = lax.dot_general(wt_ref[row:row + n, :], hb, _NT, preferred_element_type=F32)
        for u in range(nt):
            ref[u] = r[:, u * t:(u + 1) * t].astype(ref.dtype)
        row += n
    small_ref[0] = lax.dot_general(ws_ref[...], hb, _NT, preferred_element_type=F32)


def _proj(x2d, ada, g1, w_main, w_t, w_small_t, B, S):
    M, D = x2d.shape
    tm = PROJ_TM
    tpb = S // tm
    t = ATT_T
    fw = N_FOX_HEADS * HEAD_DIM
    iw = N_IDX_HEADS * IDX_DIM
    widths = [fw, fw, fw, fw, iw, LANES, D, D]
    out_shape = [jax.ShapeDtypeStruct((M, w), BF16) for w in widths]
    out_specs = [pl.BlockSpec((tm, w), lambda i: (i, 0)) for w in widths]
    for _ in range(2):
        out_shape.append(jax.ShapeDtypeStruct((M // t, fw, t), BF16))
        out_specs.append(pl.BlockSpec((tm // t, fw, t), lambda i: (i, 0, 0)))
    out_shape.append(jax.ShapeDtypeStruct((B, SMALL_ROWS, S), F32))
    out_specs.append(pl.BlockSpec((1, SMALL_ROWS, tm), lambda i: (i // tpb, 0, i % tpb)))
    return pl.pallas_call(
        _proj_kernel,
        out_shape=out_shape,
        grid=(M // tm,),
        in_specs=[pl.BlockSpec((tm, D), lambda i: (i, 0)),
                  pl.BlockSpec((1,) + ada.shape[1:], lambda i: (i // tpb, 0, 0)),
                  pl.BlockSpec((1, D), lambda i: (0, 0)),
                  _resident(w_main.shape, lambda i: (0, 0)),
                  _resident(w_t.shape, lambda i: (0, 0)),
                  _resident(w_small_t.shape, lambda i: (0, 0))],
        out_specs=out_specs,
        compiler_params=pltpu.CompilerParams(dimension_semantics=("arbitrary",),
                                             vmem_limit_bytes=VMEM_LIMIT),
        name="proj",
    )(x2d, ada, g1, w_main, w_t, w_small_t)


def _fcum_kernel(f_ref, b_ref, o_ref):
    x = f_ref[0] + b_ref[...]
    nh = x.shape[0]
    logf = -(jnp.maximum(-x, 0.0) + jnp.log1p(jnp.exp(-jnp.abs(x))))
    ch = CUM_CHUNK
    r = lax.broadcasted_iota(I32, (ch, ch), 0)
    c = lax.broadcasted_iota(I32, (ch, ch), 1)
    upper = (r <= c).astype(F32)
    carry = jnp.zeros((nh, 1), F32)
    for k in range(x.shape[1] // ch):
        seg = logf[:, k * ch:(k + 1) * ch]
        cs = jnp.dot(seg, upper, preferred_element_type=F32,
                     precision=lax.Precision.HIGHEST) + carry
        carry = cs[:, ch - 1:ch]
        rest = -cs * LOG2E
        terms = []
        for _ in range(F_TERMS):
            term = rest.astype(BF16).astype(F32)
            terms.append(term)
            rest = rest - term
        rows = jnp.concatenate(terms + [jnp.zeros((LANES - F_TERMS * nh, ch), F32)], axis=0)
        o_ref[0, k * ch:(k + 1) * ch, :] = rows.T.astype(BF16)


def _fcum(small, b_forget):
    B, _, S = small.shape
    H = N_FOX_HEADS
    return pl.pallas_call(
        _fcum_kernel,
        out_shape=jax.ShapeDtypeStruct((B, S, LANES), BF16),
        grid=(B,),
        in_specs=[pl.BlockSpec((1, H, S), lambda b: (b, 0, 0)),
                  pl.BlockSpec((H, 1), lambda b: (0, 0))],
        out_specs=pl.BlockSpec((1, S, LANES), lambda b: (b, 0, 0)),
        name="fcum",
    )(small, b_forget.reshape(H, 1))


def _softmax_stage(pair, s_ref, p_ref, a_ref, m_ref, l_ref, extra):
    t = s_ref.shape[1]
    rb = SOFTMAX_ROWS
    fold = lambda x, op: op(x.reshape(rb // SUBLANES, SUBLANES, t), axis=0)
    for e in range(2):
        h = 2 * pair + e
        if extra is not None:
            for r in range(0, t, rb):
                s_ref[e, r:r + rb, :] = extra(e, r, s_ref[e, r:r + rb, :])
        mx = None
        for r in range(0, t, rb):
            part = fold(s_ref[e, r:r + rb, :], jnp.max)
            mx = part if mx is None else jnp.maximum(mx, part)
        m_prev = m_ref[h]
        m_new = jnp.maximum(m_prev, jnp.max(mx, axis=0, keepdims=True))
        alpha = jnp.exp2(m_prev - m_new)
        total = None
        for r in range(0, t, rb):
            p = jnp.exp2(s_ref[e, r:r + rb, :] - m_new)
            part = fold(p, jnp.sum)
            total = part if total is None else total + part
            p_ref[e, r:r + rb, :] = p.astype(BF16)
        l_ref[h] = alpha * l_ref[h] + jnp.sum(total, axis=0, keepdims=True)
        m_ref[h] = m_new
        a_ref[e] = alpha


def _values_stage(pair, vt, p_ref, a_ref, acc_ref):
    for e in range(2):
        h = 2 * pair + e
        ve = vt[e * HEAD_DIM:(e + 1) * HEAD_DIM]
        acc_ref[h] = a_ref[e] * acc_ref[h] + jnp.dot(ve, p_ref[e], preferred_element_type=F32)


def _causal_attention(i, npairs, scores, softmax, values, bufs):
    s_bufs, p_bufs, a_bufs = bufs
    p_bufs[1][...] = jnp.zeros(p_bufs[1].shape, BF16)
    a_bufs[1][...] = jnp.ones(a_bufs[1].shape, F32)
    scores(0, 0, s_bufs[0])

    def tile_steps(c, near, last_tile):
        for pair in range(npairs):
            cur, oth = pair % 2, 1 - pair % 2
            softmax(pair, s_bufs[cur], p_bufs[cur], a_bufs[cur], near)
            if pair + 1 < npairs:
                scores(c, pair + 1, s_bufs[oth])
            elif not last_tile:
                scores(c + 1, 0, s_bufs[oth])
            if pair >= 1:
                values(c, pair - 1, p_bufs[oth], a_bufs[oth])
            else:
                values(jnp.maximum(c - 1, 0), npairs - 1, p_bufs[oth], a_bufs[oth])

    def far_tile(c, carry):
        tile_steps(c, None, False)
        return carry

    lax.fori_loop(0, jnp.maximum(i - 1, 0), far_tile, 0)

    @pl.when(i >= 1)
    def _():
        tile_steps(i - 1, 1, False)

    tile_steps(i, 0, True)
    last = (npairs - 1) % 2
    values(i, npairs - 1, p_bufs[last], a_bufs[last])


def _attention_scratch(t, nheads):
    pair = lambda dt: pltpu.VMEM((2, t, t), dt)
    row = pltpu.VMEM((2, 1, t), F32)
    return [pair(F32), pair(F32), pair(BF16), pair(BF16), row, row,
            pltpu.VMEM((nheads, HEAD_DIM, t), F32),
            pltpu.VMEM((nheads, 1, t), F32), pltpu.VMEM((nheads, 1, t), F32)]


def _init_softmax(m_ref, l_ref, acc_ref):
    m_ref[...] = jnp.full(m_ref.shape, NEG, F32)
    l_ref[...] = jnp.zeros(l_ref.shape, F32)
    acc_ref[...] = jnp.zeros(acc_ref.shape, F32)


def _pair_output(pair, l_ref, acc_ref):
    h = 2 * pair
    out_t = jnp.concatenate([acc_ref[h] / l_ref[h], acc_ref[h + 1] / l_ref[h + 1]], axis=0)
    return out_t.T


def _split_pair(qp, lane):
    qf = qp.astype(F32)
    return (jnp.where(lane < HEAD_DIM, qf, 0.0).astype(qp.dtype),
            jnp.where(lane >= HEAD_DIM, qf, 0.0).astype(qp.dtype))


def _fox_kernel(q_ref, k_ref, vt_ref, f_ref, o_ref, qm_ref,
                s_a, s_b, p_a, p_b, a_a, a_b, acc_ref, m_ref, l_ref):
    i = pl.program_id(1)
    t = q_ref.shape[1]
    npairs = N_FOX_HEADS // 2
    lane = lax.broadcasted_iota(I32, (t, LANES), 1)
    for pair in range(npairs):
        halves = _split_pair(q_ref[0, :, pair * LANES:(pair + 1) * LANES], lane)
        for e in range(2):
            h = 2 * pair + e
            ones = jnp.where(lane % N_FOX_HEADS == h, 1.0, 0.0)
            ones = jnp.where(lane < F_TERMS * N_FOX_HEADS, ones, 0.0).astype(BF16)
            qm_ref[h, :, 0:LANES] = halves[e]
            qm_ref[h, :, LANES:2 * LANES] = ones
    _init_softmax(m_ref, l_ref, acc_ref)

    def scores(c, pair, s_ref):
        off = pl.multiple_of(c * t, t)
        kt = jnp.concatenate([k_ref[0, pl.ds(off, t), pair * LANES:(pair + 1) * LANES],
                              f_ref[0, pl.ds(off, t), :]], axis=1)
        for e in range(2):
            s_ref[e] = lax.dot_general(kt, qm_ref[2 * pair + e], _NT,
                                       preferred_element_type=F32)

    def causal(e, r, block):
        ks = lax.broadcasted_iota(I32, block.shape, 0) + r
        qs = lax.broadcasted_iota(I32, block.shape, 1)
        return jnp.where(ks <= qs, block, NEG)

    def softmax(pair, s_ref, p_ref, a_ref, near):
        _softmax_stage(pair, s_ref, p_ref, a_ref, m_ref, l_ref, causal if near == 0 else None)

    def values(c, pair, p_ref, a_ref):
        vt = vt_ref[c, pair * LANES:(pair + 1) * LANES, :]
        _values_stage(pair, vt, p_ref, a_ref, acc_ref)

    _causal_attention(i, npairs, scores, softmax, values, ((s_a, s_b), (p_a, p_b), (a_a, a_b)))
    for pair in range(npairs):
        o_ref[0, :, pair * LANES:(pair + 1) * LANES] = _pair_output(
            pair, l_ref, acc_ref).astype(o_ref.dtype)


def _fox(qf, kf, vft, fk):
    B, S, W = qf.shape
    t = ATT_T
    nk = S // t
    return pl.pallas_call(
        _fox_kernel,
        out_shape=jax.ShapeDtypeStruct((B, S, W), BF16),
        grid=(B, nk),
        in_specs=[pl.BlockSpec((1, t, W), lambda b, i: (b, i, 0)),
                  _resident((1, S, W), lambda b, i: (b, 0, 0)),
                  _resident((nk, W, t), lambda b, i: (b, 0, 0)),
                  _resident((1, S, LANES), lambda b, i: (b, 0, 0))],
        out_specs=pl.BlockSpec((1, t, W), lambda b, i: (b, i, 0)),
        scratch_shapes=[pltpu.VMEM((N_FOX_HEADS, t, 2 * LANES), BF16)]
                       + _attention_scratch(t, N_FOX_HEADS),
        compiler_params=pltpu.CompilerParams(
            dimension_semantics=("arbitrary", "arbitrary"),
            vmem_limit_bytes=VMEM_LIMIT),
        name="fox",
    )(qf, kf, vft, fk)


def _bias_rows(tab_ref, h, t, previous, r, rows):
    nb = t // BIAS_T
    a, r0 = divmod(r, BIAS_T)
    zero = jnp.zeros((rows, BIAS_T), F32)
    blocks = []
    for b in range(nb):
        d = b - a + (nb if previous else 0)
        blocks.append(tab_ref[h, d, r0:r0 + rows, :] if d in (0, 1) else zero)
    return jnp.concatenate(blocks, axis=1)


def _dsa_kernel(qd_ref, qi_ref, w_ref, kd_ref, vt_ref, kk_ref, tab_ref, o_ref,
                sc_ref, delta_ref, qm_ref, qim_ref,
                s_a, s_b, p_a, p_b, a_a, a_b, acc_ref, m_ref, l_ref):
    i = pl.program_id(1)
    t = qd_ref.shape[1]
    S = kd_ref.shape[1]
    topk = float(min(TOPK_MAX, S // 4))
    lane = lax.broadcasted_iota(I32, (t, LANES), 1)
    ks = lax.broadcasted_iota(I32, (t, t), 0)
    qs = lax.broadcasted_iota(I32, (t, t), 1)
    causal = ks <= qs

    for p in range(N_DSA_HEADS // 2):
        a, b = _split_pair(qd_ref[0, :, p * LANES:(p + 1) * LANES], lane)
        qm_ref[2 * p] = a
        qm_ref[2 * p + 1] = b
    for p in range(N_IDX_HEADS // 2):
        a, b = _split_pair(qi_ref[0, :, p * LANES:(p + 1) * LANES], lane)
        qim_ref[2 * p] = a
        qim_ref[2 * p + 1] = b
    wrows = [w_ref[0, h:h + 1, :] * (N_IDX_HEADS ** -0.5) for h in range(N_IDX_HEADS)]

    def score_chunk(c, diag):
        kt = kk_ref[0, pl.ds(pl.multiple_of(c * t, t), t), :]
        sc = None
        for h in range(N_IDX_HEADS):
            d = lax.dot_general(kt, qim_ref[h], _NT, preferred_element_type=F32)
            term = jnp.maximum(d, 0.0) * wrows[h]
            sc = term if sc is None else sc + term
        if diag:
            sc = jnp.where(causal, sc, -jnp.inf)
        sc_ref[c] = sc

    def score_body(c, carry):
        score_chunk(c, False)
        return carry

    lax.fori_loop(0, i, score_body, 0)
    score_chunk(i, True)
    nch = i + 1

    def count(pred):
        def body(c, acc):
            v = jnp.where(pred(sc_ref[c]), 1.0, 0.0)
            return acc + jnp.sum(v.reshape(t // SUBLANES, SUBLANES, t), axis=0)
        acc = lax.fori_loop(0, nch, body, jnp.zeros((SUBLANES, t), F32))
        return jnp.sum(acc, axis=0, keepdims=True)

    def ordered_bits_to_float(u):
        return lax.bitcast_convert_type(jnp.where(u < 0, u ^ 0x7FFFFFFF, u), F32)

    def tau_step(jj, prefix):
        cand = prefix + jnp.left_shift(jnp.int32(1), 31 - jj)
        cnt = count(lambda x: x >= ordered_bits_to_float(cand))
        return jnp.where(cnt >= topk, cand, prefix)

    tau_bits = lax.fori_loop(0, 32, tau_step, jnp.full((1, t), INT_MIN, I32))
    tau = ordered_bits_to_float(tau_bits)
    q_pos = i * t + lax.broadcasted_iota(I32, (1, t), 1)
    tau = jnp.where(q_pos + 1 < int(topk), -F32_MAX, tau)

    delta_ref[...] = jnp.zeros(delta_ref.shape, F32)
    unresolved = jnp.where(count(lambda x: x > tau) >= topk, 1, 0)

    @pl.when(jnp.max(unresolved) > 0)
    def _():
        gap = ordered_bits_to_float(tau_bits + 1) - tau

        def delta_step(jj, carry):
            delta, step = carry
            cand = delta + step
            cnt = count(lambda x: x - tau >= cand)
            return jnp.where(cnt >= topk, cand, delta), step * 0.5

        delta, _ = lax.fori_loop(0, GAP_STEPS, delta_step, (jnp.zeros((1, t), F32), gap * 0.5))
        delta_ref[...] = delta

    delta = delta_ref[...]
    need = topk - count(lambda x: x - tau > delta)

    lower = jnp.where(qs < ks, 1.0, 0.0).astype(BF16)

    def mask_chunk(c, before, diag):
        d = sc_ref[c] - tau
        eq = d == delta
        eqf = jnp.where(eq, 1.0, 0.0)
        rank = jnp.dot(lower, eqf.astype(BF16), preferred_element_type=F32) + before
        tie = jnp.where(rank < need, 0.0, NEG)
        madd = jnp.where(d > delta, 0.0, jnp.where(eq, tie, NEG))
        if diag:
            madd = jnp.where(causal, madd, NEG)
        sc_ref[c] = madd
        return before + jnp.sum(eqf, axis=0, keepdims=True)

    before = lax.fori_loop(0, i, lambda c, b: mask_chunk(c, b, False), jnp.zeros((1, t), F32))
    mask_chunk(i, before, True)

    npairs = N_DSA_HEADS // 2
    _init_softmax(m_ref, l_ref, acc_ref)

    def scores(c, pair, s_ref):
        off = pl.multiple_of(c * t, t)
        kt = kd_ref[0, pl.ds(off, t), pair * LANES:(pair + 1) * LANES]
        madd = sc_ref[c]
        for e in range(2):
            s_ref[e] = lax.dot_general(kt, qm_ref[2 * pair + e], _NT,
                                       preferred_element_type=F32) + madd

    def softmax(pair, s_ref, p_ref, a_ref, near):
        bias = None
        if near is not None:
            bias = lambda e, r, block: block + _bias_rows(
                tab_ref, 2 * pair + e, t, near == 1, r, block.shape[0])
        _softmax_stage(pair, s_ref, p_ref, a_ref, m_ref, l_ref, bias)

    def values(c, pair, p_ref, a_ref):
        vt = vt_ref[c, pair * LANES:(pair + 1) * LANES, :]
        _values_stage(pair, vt, p_ref, a_ref, acc_ref)

    _causal_attention(i, npairs, scores, softmax, values, ((s_a, s_b), (p_a, p_b), (a_a, a_b)))
    for pair in range(npairs):
        o_ref[0, :, pair * LANES:(pair + 1) * LANES] = _pair_output(
            pair, l_ref, acc_ref).astype(o_ref.dtype)


def _dsa(qd, qi, small, kd, vdt, kk, tab):
    B, S, W = qd.shape
    t = ATT_T
    nk = S // t
    return pl.pallas_call(
        _dsa_kernel,
        out_shape=jax.ShapeDtypeStruct((B, S, W), BF16),
        grid=(B, nk),
        in_specs=[pl.BlockSpec((1, t, W), lambda b, i: (b, i, 0)),
                  pl.BlockSpec((1, t, qi.shape[2]), lambda b, i: (b, i, 0)),
                  pl.BlockSpec((1, SUBLANES, t), lambda b, i: (b, 1, i)),
                  _resident((1, S, W), lambda b, i: (b, 0, 0)),
                  _resident((nk, W, t), lambda b, i: (b, 0, 0)),
                  _resident((1, S, LANES), lambda b, i: (b, 0, 0)),
                  _resident(tab.shape, lambda b, i: (0, 0, 0, 0))],
        out_specs=pl.BlockSpec((1, t, W), lambda b, i: (b, i, 0)),
        scratch_shapes=[pltpu.VMEM((nk, t, t), F32),
                        pltpu.VMEM((1, t), F32),
                        pltpu.VMEM((N_DSA_HEADS, t, LANES), BF16),
                        pltpu.VMEM((N_IDX_HEADS, t, LANES), BF16)]
                       + _attention_scratch(t, N_DSA_HEADS),
        compiler_params=pltpu.CompilerParams(
            dimension_semantics=("arbitrary", "arbitrary"),
            vmem_limit_bytes=VMEM_LIMIT),
        name="dsa",
    )(qd, qi, small, kd, vdt, kk, tab)


def _post_kernel(x_ref, yf_ref, yd_ref, sgf_ref, sgd_ref, ada_ref, g2_ref, gf_ref,
                 wbf_ref, wbd_ref, wo_ref, w1_ref, w2_ref, o_ref):
    bf = jnp.dot(yf_ref[...], wbf_ref[...], preferred_element_type=F32)
    bd = jnp.dot(yd_ref[...], wbd_ref[...], preferred_element_type=F32)
    merged = sgf_ref[...].astype(F32) * bf + sgd_ref[...].astype(F32) * bd
    o = jnp.dot(merged.astype(BF16), wo_ref[...], preferred_element_type=F32)
    x1 = x_ref[...] + ada_ref[0, 2:3, :] * o
    h2 = _rms_mod(x1, g2_ref[...], ada_ref[0, 4:5, :], ada_ref[0, 3:4, :]).astype(BF16)
    acc = jnp.zeros(x1.shape, F32)
    for k in range(w1_ref.shape[1] // FF_CHUNK):
        u = jnp.dot(h2, w1_ref[:, k * FF_CHUNK:(k + 1) * FF_CHUNK], preferred_element_type=F32)
        u = jnp.square(jnp.maximum(u, 0.0)).astype(BF16)
        acc = acc + jnp.dot(u, w2_ref[k * FF_CHUNK:(k + 1) * FF_CHUNK, :],
                            preferred_element_type=F32)
    x2 = x1 + ada_ref[0, 5:6, :] * acc
    y = x2 * lax.rsqrt(jnp.mean(x2 * x2, axis=-1, keepdims=True) + EPS) * gf_ref[...]
    o_ref[...] = y


def _post(x2d, yf, yd, sgf, sgd, ada, g2, gfin, wbf, wbd, wo, w1, w2, S):
    M, D = x2d.shape
    tm = POST_TM
    tpb = S // tm
    rows = lambda w: pl.BlockSpec((tm, w), lambda i: (i, 0))
    full = lambda a: _resident(a.shape, lambda i: (0,) * a.ndim)
    return pl.pallas_call(
        _post_kernel,
        out_shape=jax.ShapeDtypeStruct((M, D), F32),
        grid=(M // tm,),
        in_specs=[rows(D), rows(yf.shape[1]), rows(yd.shape[1]), rows(D), rows(D),
                  pl.BlockSpec((1,) + ada.shape[1:], lambda i: (i // tpb, 0, 0)),
                  pl.BlockSpec((1, D), lambda i: (0, 0)),
                  pl.BlockSpec((1, D), lambda i: (0, 0)),
                  full(wbf), full(wbd), full(wo), full(w1), full(w2)],
        out_specs=rows(D),
        compiler_params=pltpu.CompilerParams(dimension_semantics=("arbitrary",),
                                             vmem_limit_bytes=VMEM_LIMIT),
        name="post",
    )(x2d, yf, yd, sgf, sgd, ada, g2, gfin, wbf, wbd, wo, w1, w2)


def kernel(x, c, w_ada, b_ada, g_norm1, w_in, b_forget, rel_bias, w_branch_fox,
           w_branch_dsa, w_out, g_norm2, w_mlp1, w_mlp2, g_final):
    B, S, D = x.shape
    fw = N_FOX_HEADS * HEAD_DIM
    dw = N_DSA_HEADS * HEAD_DIM
    iw = N_IDX_HEADS * IDX_DIM
    sizes = [fw, fw, fw, N_FOX_HEADS, dw, dw, dw, iw, IDX_DIM, N_IDX_HEADS, D, D]
    offs = [0]
    for s_ in sizes:
        offs.append(offs[-1] + s_)
    seg = lambda k: w_in[:, offs[k]:offs[k + 1]]
    q_scale = HEAD_DIM ** -0.5 * LOG2E
    i_scale = IDX_DIM ** -0.5
    w_main = jnp.concatenate(
        [seg(0) * q_scale, seg(1), seg(4) * q_scale, seg(5),
         seg(7) * i_scale, seg(8), seg(8), seg(10), seg(11)], axis=1).astype(BF16)
    w_t = jnp.concatenate([seg(2), seg(6)], axis=1).T.astype(BF16)
    w_small_t = (jnp.zeros((SMALL_ROWS, D), F32)
                 .at[:N_FOX_HEADS].set(seg(3).T)
                 .at[SUBLANES:SUBLANES + N_IDX_HEADS].set(seg(9).T)).astype(BF16)

    ada = _ada(c, w_ada, b_ada)
    tab = _biastab(rel_bias, BIAS_T)
    x2d = x.reshape(B * S, D)
    (qf, kf, qd, kd, qi, kk, sgf, sgd, vft, vdt, small) = _proj(
        x2d, ada, g_norm1.reshape(1, D), w_main, w_t, w_small_t, B, S)
    fk = _fcum(small, b_forget)
    r3 = lambda a: a.reshape(B, S, a.shape[-1])
    y_fox = _fox(r3(qf), r3(kf), vft, fk)
    y_dsa = _dsa(r3(qd), r3(qi), small, r3(kd), vdt, r3(kk), tab)
    out = _post(x2d, y_fox.reshape(B * S, fw), y_dsa.reshape(B * S, dw), sgf, sgd, ada,
                g_norm2.reshape(1, D), g_final.reshape(1, D),
                w_branch_fox.astype(BF16), w_branch_dsa.astype(BF16), w_out.astype(BF16),
                w_mlp1.astype(BF16), w_mlp2.astype(BF16), S)
    return out.reshape(B, S, D)
```

```python
import math

import jax
import jax.numpy as jnp
import numpy as np
from jax import lax
from jax.experimental import pallas as pl
from jax.experimental.pallas import tpu as pltpu

F32 = jnp.float32
BF16 = jnp.bfloat16
I32 = jnp.int32

HEAD_DIM = 64
N_FOX_HEADS = 8
N_DSA_HEADS = 8
N_IDX_HEADS = 4
IDX_DIM = 64
TOPK_MAX = 256
N_BUCKETS = 32
MAX_DISTANCE = 128
EPS = 1e-6
LOG2E = float(np.float32(math.log2(math.e)))

LANES = 128
SUBLANES = 8
INT_MIN = -2 ** 31
F32_MAX = float(np.finfo(np.float32).max)
NEG = -1e30
VMEM_LIMIT = 52 * 1024 * 1024

PROJ_TM = 512
ATT_T = 512
BIAS_T = MAX_DISTANCE
POST_TM = 512
FF_CHUNK = 1024
CUM_CHUNK = 512
SMALL_ROWS = 16
SOFTMAX_ROWS = 64
F_TERMS = 3
GAP_STEPS = 30

_NT = (((1,), (1,)), ((), ()))


def _resident(block_shape, index_map):
    return pl.BlockSpec(block_shape, index_map, pipeline_mode=pl.Buffered(1))


def _ada_kernel(c_ref, w_ref, b_ref, o_ref):
    o_ref[...] = jnp.dot(c_ref[...], w_ref[...], preferred_element_type=F32,
                         precision=lax.Precision.HIGHEST) + b_ref[...]


def _ada(c, w_ada, b_ada):
    B, D = c.shape
    n = w_ada.shape[1]
    rows = SUBLANES
    c_pad = jnp.zeros((rows, D), F32).at[:B].set(c)
    out = pl.pallas_call(
        _ada_kernel,
        out_shape=jax.ShapeDtypeStruct((rows, n), F32),
        grid=(n // D,),
        in_specs=[pl.BlockSpec((rows, D), lambda j: (0, 0)),
                  pl.BlockSpec((D, D), lambda j: (0, j)),
                  pl.BlockSpec((1, D), lambda j: (0, j))],
        out_specs=pl.BlockSpec((rows, D), lambda j: (0, j)),
        compiler_params=pltpu.CompilerParams(vmem_limit_bytes=VMEM_LIMIT),
        name="ada",
    )(c_pad, w_ada, b_ada.reshape(1, n))
    return out[:B].reshape(B, n // D, D)


def _bucket_tiles(t):
    ks = np.arange(t, dtype=np.int32)[None, :, None]
    qs = np.arange(t, dtype=np.int32)[None, None, :]
    n = np.maximum(qs - ks + np.arange(2, dtype=np.int32)[:, None, None] * t, 0)
    max_exact = N_BUCKETS // 2
    nf = np.maximum(n, 1).astype(np.float32)
    scaled = (np.log(nf / np.float32(max_exact))
              / np.float32(math.log(MAX_DISTANCE / max_exact))
              * np.float32(N_BUCKETS - max_exact)).astype(np.float32)
    large = np.minimum(max_exact + scaled.astype(np.int32), N_BUCKETS - 1)
    return np.where(n < max_exact, n, large).astype(np.int32)


def _biastab_kernel(rb_ref, bucket_ref, o_ref):
    h = pl.program_id(0)
    bucket = bucket_ref[0]
    acc = jnp.zeros(bucket.shape, F32)
    for b in range(N_BUCKETS):
        acc = jnp.where(bucket == b, rb_ref[b, h], acc)
    o_ref[0, 0] = (acc - rb_ref[N_BUCKETS - 1, h]) * LOG2E


def _biastab(rel_bias, t):
    nh = rel_bias.shape[1]
    return pl.pallas_call(
        _biastab_kernel,
        out_shape=jax.ShapeDtypeStruct((nh, 2, t, t), F32),
        grid=(nh, 2),
        in_specs=[pl.BlockSpec(memory_space=pltpu.SMEM),
                  pl.BlockSpec((1, t, t), lambda h, e: (e, 0, 0))],
        out_specs=pl.BlockSpec((1, 1, t, t), lambda h, e: (h, e, 0, 0)),
        name="biastab",
    )(rel_bias, jnp.asarray(_bucket_tiles(t)))


def _rms_mod(x, g, scale, shift):
    y = x * lax.rsqrt(jnp.mean(x * x, axis=-1, keepdims=True) + EPS) * g
    return y * (1.0 + scale) + shift


def _proj_kernel(x_ref, ada_ref, g_ref, w_ref, wt_ref, ws_ref,
                 qf_ref, kf_ref, qd_ref, kd_ref, qi_ref, kk_ref, sgf_ref, sgd_ref,
                 vft_ref, vdt_ref, small_ref):
    h = _rms_mod(x_ref[...], g_ref[...], ada_ref[0, 1:2, :], ada_ref[0, 0:1, :])
    hb = h.astype(BF16)
    col = 0
    for ref in (qf_ref, kf_ref, qd_ref, kd_ref, qi_ref, kk_ref):
        n = ref.shape[-1]
        r = jnp.dot(hb, w_ref[:, col:col + n], preferred_element_type=F32)
        ref[...] = r.astype(ref.dtype)
        col += n
    for ref in (sgf_ref, sgd_ref):
        n = ref.shape[-1]
        r = jnp.dot(hb, w_ref[:, col:col + n], preferred_element_type=F32)
        ref[...] = (1.0 / (1.0 + jnp.exp(-r))).astype(ref.dtype)
        col += n
    row = 0
    for ref in (vft_ref, vdt_ref):
        nt, n, t = ref.shape
        r = lax.dot_general(wt_ref[row:row + n, :], hb, _NT, preferred_element_type=F32)
        for u in range(nt):
            ref[u] = r[:, u * t:(u + 1) * t].astype(ref.dtype)
        row += n
    small_ref[0] = lax.dot_general(ws_ref[...], hb, _NT, preferred_element_type=F32)


def _proj(x2d, ada, g1, w_main, w_t, w_small_t, B, S):
    M, D = x2d.shape
    tm = PROJ_TM
    tpb = S // tm
    t = ATT_T
    fw = N_FOX_HEADS * HEAD_DIM
    iw = N_IDX_HEADS * IDX_DIM
    widths = [fw, fw, fw, fw, iw, LANES, D, D]
    out_shape = [jax.ShapeDtypeStruct((M, w), BF16) for w in widths]
    out_specs = [pl.BlockSpec((tm, w), lambda i: (i, 0)) for w in widths]
    for _ in range(2):
        out_shape.append(jax.ShapeDtypeStruct((M // t, fw, t), BF16))
        out_specs.append(pl.BlockSpec((tm // t, fw, t), lambda i: (i, 0, 0)))
    out_shape.append(jax.ShapeDtypeStruct((B, SMALL_ROWS, S), F32))
    out_specs.append(pl.BlockSpec((1, SMALL_ROWS, tm), lambda i: (i // tpb, 0, i % tpb)))
    return pl.pallas_call(
        _proj_kernel,
        out_shape=out_shape,
        grid=(M // tm,),
        in_specs=[pl.BlockSpec((tm, D), lambda i: (i, 0)),
                  pl.BlockSpec((1,) + ada.shape[1:], lambda i: (i // tpb, 0, 0)),
                  pl.BlockSpec((1, D), lambda i: (0, 0)),
                  _resident(w_main.shape, lambda i: (0, 0)),
                  _resident(w_t.shape, lambda i: (0, 0)),
                  _resident(w_small_t.shape, lambda i: (0, 0))],
        out_specs=out_specs,
        compiler_params=pltpu.CompilerParams(dimension_semantics=("arbitrary",),
                                             vmem_limit_bytes=VMEM_LIMIT),
        name="proj",
    )(x2d, ada, g1, w_main, w_t, w_small_t)


def _fcum_kernel(f_ref, b_ref, o_ref):
    x = f_ref[0] + b_ref[...]
    nh = x.shape[0]
    logf = -(jnp.maximum(-x, 0.0) + jnp.log1p(jnp.exp(-jnp.abs(x))))
    ch = CUM_CHUNK
    r = lax.broadcasted_iota(I32, (ch, ch), 0)
    c = lax.broadcasted_iota(I32, (ch, ch), 1)
    upper = (r <= c).astype(F32)
    carry = jnp.zeros((nh, 1), F32)
    for k in range(x.shape[1] // ch):
        seg = logf[:, k * ch:(k + 1) * ch]
        cs = jnp.dot(seg, upper, preferred_element_type=F32,
                     precision=lax.Precision.HIGHEST) + carry
        carry = cs[:, ch - 1:ch]
        rest = -cs * LOG2E
        terms = []
        for _ in range(F_TERMS):
            term = rest.astype(BF16).astype(F32)
            terms.append(term)
            rest = rest - term
        rows = jnp.concatenate(terms + [jnp.zeros((LANES - F_TERMS * nh, ch), F32)], axis=0)
        o_ref[0, k * ch:(k + 1) * ch, :] = rows.T.astype(BF16)


def _fcum(small, b_forget):
    B, _, S = small.shape
    H = N_FOX_HEADS
    return pl.pallas_call(
        _fcum_kernel,
        out_shape=jax.ShapeDtypeStruct((B, S, LANES), BF16),
        grid=(B,),
        in_specs=[pl.BlockSpec((1, H, S), lambda b: (b, 0, 0)),
                  pl.BlockSpec((H, 1), lambda b: (0, 0))],
        out_specs=pl.BlockSpec((1, S, LANES), lambda b: (b, 0, 0)),
        name="fcum",
    )(small, b_forget.reshape(H, 1))


def _tree(op, xs):
    xs = list(xs)
    while len(xs) > 1:
        xs = [op(xs[k], xs[k + 1]) if k + 1 < len(xs) else xs[k] for k in range(0, len(xs), 2)]
    return xs[0]


def _fold_rows(x, reduce):
    return reduce(x.reshape(x.shape[0] // SUBLANES, SUBLANES, x.shape[1]), axis=0)


def _column_max(block_of):
    t = ATT_T
    parts = [_fold_rows(block_of(r), jnp.max) for r in range(0, t, SOFTMAX_ROWS)]
    return jnp.max(_tree(jnp.maximum, parts), axis=0, keepdims=True)


def _scores_out(st, e, s_ref, mx_ref):
    s_ref[e] = st
    mx_ref[e] = _column_max(lambda r: st[r:r + SOFTMAX_ROWS])


def _softmax_stage(pair, s_ref, mx_ref, p_ref, a_ref, m_ref, l_ref, extra):
    t = s_ref.shape[1]
    rb = SOFTMAX_ROWS
    for e in range(2):
        h = 2 * pair + e
        if extra is not None:
            for r in range(0, t, rb):
                s_ref[e, r:r + rb, :] = extra(e, r, s_ref[e, r:r + rb, :])
        if extra is None and mx_ref is not None:
            m_cur = mx_ref[e]
        else:
            m_cur = _column_max(lambda r: s_ref[e, r:r + rb, :])
        m_prev = m_ref[h]
        m_new = jnp.maximum(m_prev, m_cur)
        alpha = jnp.exp2(m_prev - m_new)
        parts = []
        for r in range(0, t, rb):
            p = jnp.exp2(s_ref[e, r:r + rb, :] - m_new)
            parts.append(_fold_rows(p, jnp.sum))
            p_ref[e, r:r + rb, :] = p.astype(BF16)
        l_ref[h] = alpha * l_ref[h] + jnp.sum(_tree(jnp.add, parts), axis=0, keepdims=True)
        m_ref[h] = m_new
        a_ref[e] = alpha


def _values_stage(pair, vt, p_ref, a_ref, acc_ref):
    for e in range(2):
        h = 2 * pair + e
        ve = vt[e * HEAD_DIM:(e + 1) * HEAD_DIM]
        acc_ref[h] = a_ref[e] * acc_ref[h] + jnp.dot(ve, p_ref[e], preferred_element_type=F32)


def _causal_attention(i, npairs, scores, softmax, values, bufs):
    s_bufs, mx_bufs, p_bufs, a_bufs = bufs
    p_bufs[1][...] = jnp.zeros(p_bufs[1].shape, BF16)
    a_bufs[1][...] = jnp.ones(a_bufs[1].shape, F32)
    scores(0, 0, s_bufs[0], mx_bufs[0])

    def tile_steps(c, near, last_tile):
        for pair in range(npairs):
            cur, oth = pair % 2, 1 - pair % 2
            softmax(pair, s_bufs[cur], mx_bufs[cur], p_bufs[cur], a_bufs[cur], near)
            if pair + 1 < npairs:
                scores(c, pair + 1, s_bufs[oth], mx_bufs[oth])
            elif not last_tile:
                scores(c + 1, 0, s_bufs[oth], mx_bufs[oth])
            if pair >= 1:
                values(c, pair - 1, p_bufs[oth], a_bufs[oth])
            else:
                values(jnp.maximum(c - 1, 0), npairs - 1, p_bufs[oth], a_bufs[oth])

    def far_tile(c, carry):
        tile_steps(c, None, False)
        return carry

    lax.fori_loop(0, jnp.maximum(i - 1, 0), far_tile, 0)

    @pl.when(i >= 1)
    def _():
        tile_steps(i - 1, 1, False)

    tile_steps(i, 0, True)
    last = (npairs - 1) % 2
    values(i, npairs - 1, p_bufs[last], a_bufs[last])


def _attention_scratch(t, nheads):
    pair = lambda dt: pltpu.VMEM((2, t, t), dt)
    row = pltpu.VMEM((2, 1, t), F32)
    return [pair(F32), pair(F32), row, row,
            pair(BF16), pair(BF16), row, row,
            pltpu.VMEM((nheads, HEAD_DIM, t), F32),
            pltpu.VMEM((nheads, 1, t), F32), pltpu.VMEM((nheads, 1, t), F32)]


def _init_softmax(m_ref, l_ref, acc_ref):
    m_ref[...] = jnp.full(m_ref.shape, NEG, F32)
    l_ref[...] = jnp.zeros(l_ref.shape, F32)
    acc_ref[...] = jnp.zeros(acc_ref.shape, F32)


def _pair_output(pair, l_ref, acc_ref):
    h = 2 * pair
    out_t = jnp.concatenate([acc_ref[h] / l_ref[h], acc_ref[h + 1] / l_ref[h + 1]], axis=0)
    return out_t.T


def _split_pair(qp, lane):
    qf = qp.astype(F32)
    return (jnp.where(lane < HEAD_DIM, qf, 0.0).astype(qp.dtype),
            jnp.where(lane >= HEAD_DIM, qf, 0.0).astype(qp.dtype))


def _fox_kernel(q_ref, k_ref, vt_ref, f_ref, o_ref, qm_ref,
                s_a, s_b, mx_a, mx_b, p_a, p_b, a_a, a_b, acc_ref, m_ref, l_ref):
    i = pl.program_id(1)
    t = q_ref.shape[1]
    npairs = N_FOX_HEADS // 2
    lane = lax.broadcasted_iota(I32, (t, LANES), 1)
    for pair in range(npairs):
        halves = _split_pair(q_ref[0, :, pair * LANES:(pair + 1) * LANES], lane)
        for e in range(2):
            h = 2 * pair + e
            ones = jnp.where(lane % N_FOX_HEADS == h, 1.0, 0.0)
            ones = jnp.where(lane < F_TERMS * N_FOX_HEADS, ones, 0.0).astype(BF16)
            qm_ref[h, :, 0:LANES] = halves[e]
            qm_ref[h, :, LANES:2 * LANES] = ones
    _init_softmax(m_ref, l_ref, acc_ref)

    def scores(c, pair, s_ref, mx_ref):
        off = pl.multiple_of(c * t, t)
        kt = jnp.concatenate([k_ref[0, pl.ds(off, t), pair * LANES:(pair + 1) * LANES],
                              f_ref[0, pl.ds(off, t), :]], axis=1)
        for e in range(2):
            st = lax.dot_general(kt, qm_ref[2 * pair + e], _NT, preferred_element_type=F32)
            _scores_out(st, e, s_ref, mx_ref)

    def causal(e, r, block):
        ks = lax.broadcasted_iota(I32, block.shape, 0) + r
        qs = lax.broadcasted_iota(I32, block.shape, 1)
        return jnp.where(ks <= qs, block, NEG)

    def softmax(pair, s_ref, mx_ref, p_ref, a_ref, near):
        _softmax_stage(pair, s_ref, mx_ref, p_ref, a_ref, m_ref, l_ref, causal if near == 0 else None)

    def values(c, pair, p_ref, a_ref):
        vt = vt_ref[c, pair * LANES:(pair + 1) * LANES, :]
        _values_stage(pair, vt, p_ref, a_ref, acc_ref)

    _causal_attention(i, npairs, scores, softmax, values, ((s_a, s_b), (mx_a, mx_b), (p_a, p_b), (a_a, a_b)))
    for pair in range(npairs):
        o_ref[0, :, pair * LANES:(pair + 1) * LANES] = _pair_output(
            pair, l_ref, acc_ref).astype(o_ref.dtype)


def _fox(qf, kf, vft, fk):
    B, S, W = qf.shape
    t = ATT_T
    nk = S // t
    return pl.pallas_call(
        _fox_kernel,
        out_shape=jax.ShapeDtypeStruct((B, S, W), BF16),
        grid=(B, nk),
        in_specs=[pl.BlockSpec((1, t, W), lambda b, i: (b, i, 0)),
                  _resident((1, S, W), lambda b, i: (b, 0, 0)),
                  _resident((nk, W, t), lambda b, i: (b, 0, 0)),
                  _resident((1, S, LANES), lambda b, i: (b, 0, 0))],
        out_specs=pl.BlockSpec((1, t, W), lambda b, i: (b, i, 0)),
        scratch_shapes=[pltpu.VMEM((N_FOX_HEADS, t, 2 * LANES), BF16)]
                       + _attention_scratch(t, N_FOX_HEADS),
        compiler_params=pltpu.CompilerParams(
            dimension_semantics=("arbitrary", "arbitrary"),
            vmem_limit_bytes=VMEM_LIMIT),
        name="fox",
    )(qf, kf, vft, fk)


def _bias_rows(tab_ref, h, t, previous, r, rows):
    nb = t // BIAS_T
    a, r0 = divmod(r, BIAS_T)
    zero = jnp.zeros((rows, BIAS_T), F32)
    blocks = []
    for b in range(nb):
        d = b - a + (nb if previous else 0)
        blocks.append(tab_ref[h, d, r0:r0 + rows, :] if d in (0, 1) else zero)
    return jnp.concatenate(blocks, axis=1)


def _dsa_kernel(qd_ref, qi_ref, w_ref, kd_ref, vt_ref, kk_ref, tab_ref, o_ref,
                sc_ref, delta_ref, qm_ref, qim_ref,
                s_a, s_b, mx_a, mx_b, p_a, p_b, a_a, a_b, acc_ref, m_ref, l_ref):
    i = pl.program_id(1)
    t = qd_ref.shape[1]
    S = kd_ref.shape[1]
    topk = float(min(TOPK_MAX, S // 4))
    lane = lax.broadcasted_iota(I32, (t, LANES), 1)
    ks = lax.broadcasted_iota(I32, (t, t), 0)
    qs = lax.broadcasted_iota(I32, (t, t), 1)
    causal = ks <= qs

    for p in range(N_DSA_HEADS // 2):
        a, b = _split_pair(qd_ref[0, :, p * LANES:(p + 1) * LANES], lane)
        qm_ref[2 * p] = a
        qm_ref[2 * p + 1] = b
    for p in range(N_IDX_HEADS // 2):
        a, b = _split_pair(qi_ref[0, :, p * LANES:(p + 1) * LANES], lane)
        qim_ref[2 * p] = a
        qim_ref[2 * p + 1] = b
    wrows = [w_ref[0, h:h + 1, :] * (N_IDX_HEADS ** -0.5) for h in range(N_IDX_HEADS)]

    def score_chunk(c, diag):
        kt = kk_ref[0, pl.ds(pl.multiple_of(c * t, t), t), :]
        sc = None
        for h in range(N_IDX_HEADS):
            d = lax.dot_general(kt, qim_ref[h], _NT, preferred_element_type=F32)
            term = jnp.maximum(d, 0.0) * wrows[h]
            sc = term if sc is None else sc + term
        if diag:
            sc = jnp.where(causal, sc, -jnp.inf)
        sc_ref[c] = sc

    def score_body(c, carry):
        score_chunk(c, False)
        return carry

    lax.fori_loop(0, i, score_body, 0)
    score_chunk(i, True)
    nch = i + 1

    def count(pred):
        def body(c, acc):
            v = jnp.where(pred(sc_ref[c]), 1.0, 0.0)
            return acc + jnp.sum(v.reshape(t // SUBLANES, SUBLANES, t), axis=0)
        acc = lax.fori_loop(0, nch, body, jnp.zeros((SUBLANES, t), F32))
        return jnp.sum(acc, axis=0, keepdims=True)

    def ordered_bits_to_float(u):
        return lax.bitcast_convert_type(jnp.where(u < 0, u ^ 0x7FFFFFFF, u), F32)

    def tau_step(jj, prefix):
        cand = prefix + jnp.left_shift(jnp.int32(1), 31 - jj)
        cnt = count(lambda x: x >= ordered_bits_to_float(cand))
        return jnp.where(cnt >= topk, cand, prefix)

    tau_bits = lax.fori_loop(0, 32, tau_step, jnp.full((1, t), INT_MIN, I32))
    tau = ordered_bits_to_float(tau_bits)
    q_pos = i * t + lax.broadcasted_iota(I32, (1, t), 1)
    tau = jnp.where(q_pos + 1 < int(topk), -F32_MAX, tau)

    above = count(lambda x: x - tau > 0.0)
    delta_ref[0:1, :] = jnp.zeros((1, t), F32)
    delta_ref[1:2, :] = topk - above

    @pl.when(jnp.max(jnp.where(above >= topk, 1, 0)) > 0)
    def _():
        gap = ordered_bits_to_float(tau_bits + 1) - tau

        def delta_step(jj, carry):
            delta, step = carry
            cand = delta + step
            cnt = count(lambda x: x - tau >= cand)
            return jnp.where(cnt >= topk, cand, delta), step * 0.5

        delta, _ = lax.fori_loop(0, GAP_STEPS, delta_step, (jnp.zeros((1, t), F32), gap * 0.5))
        delta_ref[0:1, :] = delta
        delta_ref[1:2, :] = topk - count(lambda x: x - tau > delta)

    delta = delta_ref[0:1, :]
    need = delta_ref[1:2, :]

    lower = jnp.where(qs < ks, 1.0, 0.0).astype(BF16)

    def mask_chunk(c, before, diag):
        d = sc_ref[c] - tau
        eq = d == delta
        eqf = jnp.where(eq, 1.0, 0.0)
        rank = jnp.dot(lower, eqf.astype(BF16), preferred_element_type=F32) + before
        tie = jnp.where(rank < need, 0.0, NEG)
        madd = jnp.where(d > delta, 0.0, jnp.where(eq, tie, NEG))
        if diag:
            madd = jnp.where(causal, madd, NEG)
        sc_ref[c] = madd
        return before + jnp.sum(eqf, axis=0, keepdims=True)

    before = lax.fori_loop(0, i, lambda c, b: mask_chunk(c, b, False), jnp.zeros((1, t), F32))
    mask_chunk(i, before, True)

    npairs = N_DSA_HEADS // 2
    _init_softmax(m_ref, l_ref, acc_ref)

    def scores(c, pair, s_ref, mx_ref):
        off = pl.multiple_of(c * t, t)
        kt = kd_ref[0, pl.ds(off, t), pair * LANES:(pair + 1) * LANES]
        madd = sc_ref[c]
        for e in range(2):
            s_ref[e] = lax.dot_general(kt, qm_ref[2 * pair + e], _NT,
                                       preferred_element_type=F32) + madd

    def softmax(pair, s_ref, mx_ref, p_ref, a_ref, near):
        bias = None
        if near is not None:
            bias = lambda e, r, block: block + _bias_rows(
                tab_ref, 2 * pair + e, t, near == 1, r, block.shape[0])
        _softmax_stage(pair, s_ref, None, p_ref, a_ref, m_ref, l_ref, bias)

    def values(c, pair, p_ref, a_ref):
        vt = vt_ref[c, pair * LANES:(pair + 1) * LANES, :]
        _values_stage(pair, vt, p_ref, a_ref, acc_ref)

    _causal_attention(i, npairs, scores, softmax, values, ((s_a, s_b), (mx_a, mx_b), (p_a, p_b), (a_a, a_b)))
    for pair in range(npairs):
        o_ref[0, :, pair * LANES:(pair + 1) * LANES] = _pair_output(
            pair, l_ref, acc_ref).astype(o_ref.dtype)


def _dsa(qd, qi, small, kd, vdt, kk, tab):
    B, S, W = qd.shape
    t = ATT_T
    nk = S // t
    return pl.pallas_call(
        _dsa_kernel,
        out_shape=jax.ShapeDtypeStruct((B, S, W), BF16),
        grid=(B, nk),
        in_specs=[pl.BlockSpec((1, t, W), lambda b, i: (b, i, 0)),
                  pl.BlockSpec((1, t, qi.shape[2]), lambda b, i: (b, i, 0)),
                  pl.BlockSpec((1, SUBLANES, t), lambda b, i: (b, 1, i)),
                  _resident((1, S, W), lambda b, i: (b, 0, 0)),
                  _resident((nk, W, t), lambda b, i: (b, 0, 0)),
                  _resident((1, S, LANES), lambda b, i: (b, 0, 0)),
                  _resident(tab.shape, lambda b, i: (0, 0, 0, 0))],
        out_specs=pl.BlockSpec((1, t, W), lambda b, i: (b, i, 0)),
        scratch_shapes=[pltpu.VMEM((nk, t, t), F32),
                        pltpu.VMEM((2, t), F32),
                        pltpu.VMEM((N_DSA_HEADS, t, LANES), BF16),
                        pltpu.VMEM((N_IDX_HEADS, t, LANES), BF16)]
                       + _attention_scratch(t, N_DSA_HEADS),
        compiler_params=pltpu.CompilerParams(
            dimension_semantics=("arbitrary", "arbitrary"),
            vmem_limit_bytes=VMEM_LIMIT),
        name="dsa",
    )(qd, qi, small, kd, vdt, kk, tab)


def _post_kernel(x_ref, yf_ref, yd_ref, sgf_ref, sgd_ref, ada_ref, g2_ref, gf_ref,
                 wbf_ref, wbd_ref, wo_ref, w1_ref, w2_ref, o_ref):
    bf = jnp.dot(yf_ref[...], wbf_ref[...], preferred_element_type=F32)
    bd = jnp.dot(yd_ref[...], wbd_ref[...], preferred_element_type=F32)
    merged = sgf_ref[...].astype(F32) * bf + sgd_ref[...].astype(F32) * bd
    o = jnp.dot(merged.astype(BF16), wo_ref[...], preferred_element_type=F32)
    x1 = x_ref[...] + ada_ref[0, 2:3, :] * o
    h2 = _rms_mod(x1, g2_ref[...], ada_ref[0, 4:5, :], ada_ref[0, 3:4, :]).astype(BF16)
    acc = jnp.zeros(x1.shape, F32)
    for k in range(w1_ref.shape[1] // FF_CHUNK):
        u = jnp.dot(h2, w1_ref[:, k * FF_CHUNK:(k + 1) * FF_CHUNK], preferred_element_type=F32)
        u = jnp.square(jnp.maximum(u, 0.0)).astype(BF16)
        acc = acc + jnp.dot(u, w2_ref[k * FF_CHUNK:(k + 1) * FF_CHUNK, :],
                            preferred_element_type=F32)
    x2 = x1 + ada_ref[0, 5:6, :] * acc
    y = x2 * lax.rsqrt(jnp.mean(x2 * x2, axis=-1, keepdims=True) + EPS) * gf_ref[...]
    o_ref[...] = y


def _post(x2d, yf, yd, sgf, sgd, ada, g2, gfin, wbf, wbd, wo, w1, w2, S):
    M, D = x2d.shape
    tm = POST_TM
    tpb = S // tm
    rows = lambda w: pl.BlockSpec((tm, w), lambda i: (i, 0))
    full = lambda a: _resident(a.shape, lambda i: (0,) * a.ndim)
    return pl.pallas_call(
        _post_kernel,
        out_shape=jax.ShapeDtypeStruct((M, D), F32),
        grid=(M // tm,),
        in_specs=[rows(D), rows(yf.shape[1]), rows(yd.shape[1]), rows(D), rows(D),
                  pl.BlockSpec((1,) + ada.shape[1:], lambda i: (i // tpb, 0, 0)),
                  pl.BlockSpec((1, D), lambda i: (0, 0)),
                  pl.BlockSpec((1, D), lambda i: (0, 0)),
                  full(wbf), full(wbd), full(wo), full(w1), full(w2)],
        out_specs=rows(D),
        compiler_params=pltpu.CompilerParams(dimension_semantics=("arbitrary",),
                                             vmem_limit_bytes=VMEM_LIMIT),
        name="post",
    )(x2d, yf, yd, sgf, sgd, ada, g2, gfin, wbf, wbd, wo, w1, w2)


def kernel(x, c, w_ada, b_ada, g_norm1, w_in, b_forget, rel_bias, w_branch_fox,
           w_branch_dsa, w_out, g_norm2, w_mlp1, w_mlp2, g_final):
    B, S, D = x.shape
    fw = N_FOX_HEADS * HEAD_DIM
    dw = N_DSA_HEADS * HEAD_DIM
    iw = N_IDX_HEADS * IDX_DIM
    sizes = [fw, fw, fw, N_FOX_HEADS, dw, dw, dw, iw, IDX_DIM, N_IDX_HEADS, D, D]
    offs = [0]
    for s_ in sizes:
        offs.append(offs[-1] + s_)
    seg = lambda k: w_in[:, offs[k]:offs[k + 1]]
    q_scale = HEAD_DIM ** -0.5 * LOG2E
    i_scale = IDX_DIM ** -0.5
    w_main = jnp.concatenate(
        [seg(0) * q_scale, seg(1), seg(4) * q_scale, seg(5),
         seg(7) * i_scale, seg(8), seg(8), seg(10), seg(11)], axis=1).astype(BF16)
    w_t = jnp.concatenate([seg(2), seg(6)], axis=1).T.astype(BF16)
    w_small_t = (jnp.zeros((SMALL_ROWS, D), F32)
                 .at[:N_FOX_HEADS].set(seg(3).T)
                 .at[SUBLANES:SUBLANES + N_IDX_HEADS].set(seg(9).T)).astype(BF16)

    ada = _ada(c, w_ada, b_ada)
    tab = _biastab(rel_bias, BIAS_T)
    x2d = x.reshape(B * S, D)
    (qf, kf, qd, kd, qi, kk, sgf, sgd, vft, vdt, small) = _proj(
        x2d, ada, g_norm1.reshape(1, D), w_main, w_t, w_small_t, B, S)
    fk = _fcum(small, b_forget)
    r3 = lambda a: a.reshape(B, S, a.shape[-1])
    y_fox = _fox(r3(qf), r3(kf), vft, fk)
    y_dsa = _dsa(r3(qd), r3(qi), small, r3(kd), vdt, r3(kk), tab)
    out = _post(x2d, y_fox.reshape(B * S, fw), y_dsa.reshape(B * S, dw), sgf, sgd, ada,
                g_norm2.reshape(1, D), g_final.reshape(1, D),
                w_branch_fox.astype(BF16), w_branch_dsa.astype(BF16), w_out.astype(BF16),
                w_mlp1.astype(BF16), w_mlp2.astype(BF16), S)
    return out.reshape(B, S, D)
```

```python
import math

import jax
import jax.numpy as jnp
import numpy as np
from jax import lax
from jax.experimental import pallas as pl
from jax.experimental.pallas import tpu as pltpu

F32 = jnp.float32
BF16 = jnp.bfloat16
I32 = jnp.int32

HEAD_DIM = 64
N_FOX_HEADS = 8
N_DSA_HEADS = 8
N_IDX_HEADS = 4
IDX_DIM = 64
TOPK_MAX = 256
N_BUCKETS = 32
MAX_DISTANCE = 128
EPS = 1e-6
LOG2E = float(np.float32(math.log2(math.e)))

LANES = 128
SUBLANES = 8
PACKED_ROWS = 16
INT_MIN = -2 ** 31
INT_MAX = 2 ** 31 - 1
HALF_SPAN = 2 ** 15
NEG_INF_BITS = INT_MIN + 0x7FFFFF
RANGE_STEPS = 18
F32_MAX = float(np.finfo(np.float32).max)
NEG = -1e30
VMEM_LIMIT = 52 * 1024 * 1024

PROJ_TM = 512
ATT_T = 512
BIAS_T = MAX_DISTANCE
POST_TM = 512
FF_CHUNK = 1024
CUM_CHUNK = 512
SMALL_ROWS = 16
SOFTMAX_ROWS = 64
F_TERMS = 3
GAP_STEPS = 30

_NT = (((1,), (1,)), ((), ()))


def _resident(block_shape, index_map):
    return pl.BlockSpec(block_shape, index_map, pipeline_mode=pl.Buffered(1))


def _ada_kernel(c_ref, w_ref, b_ref, o_ref):
    o_ref[...] = jnp.dot(c_ref[...], w_ref[...], preferred_element_type=F32,
                         precision=lax.Precision.HIGHEST) + b_ref[...]


def _ada(c, w_ada, b_ada):
    B, D = c.shape
    n = w_ada.shape[1]
    rows = SUBLANES
    c_pad = jnp.zeros((rows, D), F32).at[:B].set(c)
    out = pl.pallas_call(
        _ada_kernel,
        out_shape=jax.ShapeDtypeStruct((rows, n), F32),
        grid=(n // D,),
        in_specs=[pl.BlockSpec((rows, D), lambda j: (0, 0)),
                  pl.BlockSpec((D, D), lambda j: (0, j)),
                  pl.BlockSpec((1, D), lambda j: (0, j))],
        out_specs=pl.BlockSpec((rows, D), lambda j: (0, j)),
        compiler_params=pltpu.CompilerParams(vmem_limit_bytes=VMEM_LIMIT),
        name="ada",
    )(c_pad, w_ada, b_ada.reshape(1, n))
    return out[:B].reshape(B, n // D, D)


def _bucket_tiles(t):
    ks = np.arange(t, dtype=np.int32)[None, :, None]
    qs = np.arange(t, dtype=np.int32)[None, None, :]
    n = np.maximum(qs - ks + np.arange(2, dtype=np.int32)[:, None, None] * t, 0)
    max_exact = N_BUCKETS // 2
    nf = np.maximum(n, 1).astype(np.float32)
    scaled = (np.log(nf / np.float32(max_exact))
              / np.float32(math.log(MAX_DISTANCE / max_exact))
              * np.float32(N_BUCKETS - max_exact)).astype(np.float32)
    large = np.minimum(max_exact + scaled.astype(np.int32), N_BUCKETS - 1)
    return np.where(n < max_exact, n, large).astype(np.int32)


def _biastab_kernel(rb_ref, bucket_ref, o_ref):
    h = pl.program_id(0)
    bucket = bucket_ref[0]
    acc = jnp.zeros(bucket.shape, F32)
    for b in range(N_BUCKETS):
        acc = jnp.where(bucket == b, rb_ref[b, h], acc)
    o_ref[0, 0] = (acc - rb_ref[N_BUCKETS - 1, h]) * LOG2E


def _biastab(rel_bias, t):
    nh = rel_bias.shape[1]
    return pl.pallas_call(
        _biastab_kernel,
        out_shape=jax.ShapeDtypeStruct((nh, 2, t, t), F32),
        grid=(nh, 2),
        in_specs=[pl.BlockSpec(memory_space=pltpu.SMEM),
                  pl.BlockSpec((1, t, t), lambda h, e: (e, 0, 0))],
        out_specs=pl.BlockSpec((1, 1, t, t), lambda h, e: (h, e, 0, 0)),
        name="biastab",
    )(rel_bias, jnp.asarray(_bucket_tiles(t)))


def _rms_mod(x, g, scale, shift):
    y = x * lax.rsqrt(jnp.mean(x * x, axis=-1, keepdims=True) + EPS) * g
    return y * (1.0 + scale) + shift


def _proj_kernel(x_ref, ada_ref, g_ref, w_ref, wt_ref, ws_ref,
                 qf_ref, kf_ref, qd_ref, kd_ref, qi_ref, kk_ref, sgf_ref, sgd_ref,
                 vft_ref, vdt_ref, small_ref):
    h = _rms_mod(x_ref[...], g_ref[...], ada_ref[0, 1:2, :], ada_ref[0, 0:1, :])
    hb = h.astype(BF16)
    col = 0
    for ref in (qf_ref, kf_ref, qd_ref, kd_ref, qi_ref, kk_ref):
        n = ref.shape[-1]
        r = jnp.dot(hb, w_ref[:, col:col + n], preferred_element_type=F32)
        ref[...] = r.astype(ref.dtype)
        col += n
    for ref in (sgf_ref, sgd_ref):
        n = ref.shape[-1]
        r = jnp.dot(hb, w_ref[:, col:col + n], preferred_element_type=F32)
        ref[...] = (1.0 / (1.0 + jnp.exp(-r))).astype(ref.dtype)
        col += n
    row = 0
    for ref in (vft_ref, vdt_ref):
        nt, n, t = ref.shape
        r = lax.dot_general(wt_ref[row:row + n, :], hb, _NT, preferred_element_type=F32)
        for u in range(nt):
            ref[u] = r[:, u * t:(u + 1) * t].astype(ref.dtype)
        row += n
    small_ref[0] = lax.dot_general(ws_ref[...], hb, _NT, preferred_element_type=F32)


def _proj(x2d, ada, g1, w_main, w_t, w_small_t, B, S):
    M, D = x2d.shape
    tm = PROJ_TM
    tpb = S // tm
    t = ATT_T
    fw = N_FOX_HEADS * HEAD_DIM
    iw = N_IDX_HEADS * IDX_DIM
    widths = [fw, fw, fw, fw, iw, LANES, D, D]
    out_shape = [jax.ShapeDtypeStruct((M, w), BF16) for w in widths]
    out_specs = [pl.BlockSpec((tm, w), lambda i: (i, 0)) for w in widths]
    for _ in range(2):
        out_shape.append(jax.ShapeDtypeStruct((M // t, fw, t), BF16))
        out_specs.append(pl.BlockSpec((tm // t, fw, t), lambda i: (i, 0, 0)))
    out_shape.append(jax.ShapeDtypeStruct((B, SMALL_ROWS, S), F32))
    out_specs.append(pl.BlockSpec((1, SMALL_ROWS, tm), lambda i: (i // tpb, 0, i % tpb)))
    return pl.pallas_call(
        _proj_kernel,
        out_shape=out_shape,
        grid=(M // tm,),
        in_specs=[pl.BlockSpec((tm, D), lambda i: (i, 0)),
                  pl.BlockSpec((1,) + ada.shape[1:], lambda i: (i // tpb, 0, 0)),
                  pl.BlockSpec((1, D), lambda i: (0, 0)),
                  _resident(w_main.shape, lambda i: (0, 0)),
                  _resident(w_t.shape, lambda i: (0, 0)),
                  _resident(w_small_t.shape, lambda i: (0, 0))],
        out_specs=out_specs,
        compiler_params=pltpu.CompilerParams(dimension_semantics=("arbitrary",),
                                             vmem_limit_bytes=VMEM_LIMIT),
        name="proj",
    )(x2d, ada, g1, w_main, w_t, w_small_t)


def _fcum_kernel(f_ref, b_ref, o_ref):
    x = f_ref[0] + b_ref[...]
    nh = x.shape[0]
    logf = -(jnp.maximum(-x, 0.0) + jnp.log1p(jnp.exp(-jnp.abs(x))))
    ch = CUM_CHUNK
    r = lax.broadcasted_iota(I32, (ch, ch), 0)
    c = lax.broadcasted_iota(I32, (ch, ch), 1)
    upper = (r <= c).astype(F32)
    carry = jnp.zeros((nh, 1), F32)
    for k in range(x.shape[1] // ch):
        seg = logf[:, k * ch:(k + 1) * ch]
        cs = jnp.dot(seg, upper, preferred_element_type=F32,
                     precision=lax.Precision.HIGHEST) + carry
        carry = cs[:, ch - 1:ch]
        rest = -cs * LOG2E
        terms = []
        for _ in range(F_TERMS):
            term = rest.astype(BF16).astype(F32)
            terms.append(term)
            rest = rest - term
        rows = jnp.concatenate(terms + [jnp.zeros((LANES - F_TERMS * nh, ch), F32)], axis=0)
        o_ref[0, k * ch:(k + 1) * ch, :] = rows.T.astype(BF16)


def _fcum(small, b_forget):
    B, _, S = small.shape
    H = N_FOX_HEADS
    return pl.pallas_call(
        _fcum_kernel,
        out_shape=jax.ShapeDtypeStruct((B, S, LANES), BF16),
        grid=(B,),
        in_specs=[pl.BlockSpec((1, H, S), lambda b: (b, 0, 0)),
                  pl.BlockSpec((H, 1), lambda b: (0, 0))],
        out_specs=pl.BlockSpec((1, S, LANES), lambda b: (b, 0, 0)),
        name="fcum",
    )(small, b_forget.reshape(H, 1))


def _tree(op, xs):
    xs = list(xs)
    while len(xs) > 1:
        xs = [op(xs[k], xs[k + 1]) if k + 1 < len(xs) else xs[k] for k in range(0, len(xs), 2)]
    return xs[0]


def _fold_rows(x, reduce):
    return reduce(x.reshape(x.shape[0] // SUBLANES, SUBLANES, x.shape[1]), axis=0)


def _column_max(block_of):
    t = ATT_T
    parts = [_fold_rows(block_of(r), jnp.max) for r in range(0, t, SOFTMAX_ROWS)]
    return jnp.max(_tree(jnp.maximum, parts), axis=0, keepdims=True)


def _scores_out(st, e, s_ref, mx_ref):
    s_ref[e] = st
    mx_ref[e] = _column_max(lambda r: st[r:r + SOFTMAX_ROWS])


def _softmax_stage(pair, s_ref, mx_ref, p_ref, a_ref, m_ref, l_ref, extra):
    t = s_ref.shape[1]
    rb = SOFTMAX_ROWS
    for e in range(2):
        h = 2 * pair + e
        if extra is not None:
            for r in range(0, t, rb):
                s_ref[e, r:r + rb, :] = extra(e, r, s_ref[e, r:r + rb, :])
        if extra is None and mx_ref is not None:
            m_cur = mx_ref[e]
        else:
            m_cur = _column_max(lambda r: s_ref[e, r:r + rb, :])
        m_prev = m_ref[h]
        m_new = jnp.maximum(m_prev, m_cur)
        alpha = jnp.exp2(m_prev - m_new)
        parts = []
        for r in range(0, t, rb):
            p = jnp.exp2(s_ref[e, r:r + rb, :] - m_new)
            parts.append(_fold_rows(p, jnp.sum))
            p_ref[e, r:r + rb, :] = p.astype(BF16)
        l_ref[h] = alpha * l_ref[h] + jnp.sum(_tree(jnp.add, parts), axis=0, keepdims=True)
        m_ref[h] = m_new
        a_ref[e] = alpha


def _values_stage(pair, vt, p_ref, a_ref, acc_ref):
    for e in range(2):
        h = 2 * pair + e
        ve = vt[e * HEAD_DIM:(e + 1) * HEAD_DIM]
        acc_ref[h] = a_ref[e] * acc_ref[h] + jnp.dot(ve, p_ref[e], preferred_element_type=F32)


def _causal_attention(i, npairs, scores, softmax, values, bufs):
    s_bufs, mx_bufs, p_bufs, a_bufs = bufs
    p_bufs[1][...] = jnp.zeros(p_bufs[1].shape, BF16)
    a_bufs[1][...] = jnp.ones(a_bufs[1].shape, F32)
    scores(0, 0, s_bufs[0], mx_bufs[0])

    def tile_steps(c, near, last_tile):
        for pair in range(npairs):
            cur, oth = pair % 2, 1 - pair % 2
            softmax(pair, s_bufs[cur], mx_bufs[cur], p_bufs[cur], a_bufs[cur], near)
            if pair + 1 < npairs:
                scores(c, pair + 1, s_bufs[oth], mx_bufs[oth])
            elif not last_tile:
                scores(c + 1, 0, s_bufs[oth], mx_bufs[oth])
            if pair >= 1:
                values(c, pair - 1, p_bufs[oth], a_bufs[oth])
            else:
                values(jnp.maximum(c - 1, 0), npairs - 1, p_bufs[oth], a_bufs[oth])

    def far_tile(c, carry):
        tile_steps(c, None, False)
        return carry

    lax.fori_loop(0, jnp.maximum(i - 1, 0), far_tile, 0)

    @pl.when(i >= 1)
    def _():
        tile_steps(i - 1, 1, False)

    tile_steps(i, 0, True)
    last = (npairs - 1) % 2
    values(i, npairs - 1, p_bufs[last], a_bufs[last])


def _attention_scratch(t, nheads):
    pair = lambda dt: pltpu.VMEM((2, t, t), dt)
    row = pltpu.VMEM((2, 1, t), F32)
    return [pair(F32), pair(F32), row, row,
            pair(BF16), pair(BF16), row, row,
            pltpu.VMEM((nheads, HEAD_DIM, t), F32),
            pltpu.VMEM((nheads, 1, t), F32), pltpu.VMEM((nheads, 1, t), F32)]


def _init_softmax(m_ref, l_ref, acc_ref):
    m_ref[...] = jnp.full(m_ref.shape, NEG, F32)
    l_ref[...] = jnp.zeros(l_ref.shape, F32)
    acc_ref[...] = jnp.zeros(acc_ref.shape, F32)


def _pair_output(pair, l_ref, acc_ref):
    h = 2 * pair
    out_t = jnp.concatenate([acc_ref[h] / l_ref[h], acc_ref[h + 1] / l_ref[h + 1]], axis=0)
    return out_t.T


def _split_pair(qp, lane):
    qf = qp.astype(F32)
    return (jnp.where(lane < HEAD_DIM, qf, 0.0).astype(qp.dtype),
            jnp.where(lane >= HEAD_DIM, qf, 0.0).astype(qp.dtype))


def _fox_kernel(q_ref, k_ref, vt_ref, f_ref, o_ref, qm_ref,
                s_a, s_b, mx_a, mx_b, p_a, p_b, a_a, a_b, acc_ref, m_ref, l_ref):
    i = pl.program_id(1)
    t = q_ref.shape[1]
    npairs = N_FOX_HEADS // 2
    lane = lax.broadcasted_iota(I32, (t, LANES), 1)
    for pair in range(npairs):
        halves = _split_pair(q_ref[0, :, pair * LANES:(pair + 1) * LANES], lane)
        for e in range(2):
            h = 2 * pair + e
            ones = jnp.where(lane % N_FOX_HEADS == h, 1.0, 0.0)
            ones = jnp.where(lane < F_TERMS * N_FOX_HEADS, ones, 0.0).astype(BF16)
            qm_ref[h, :, 0:LANES] = halves[e]
            qm_ref[h, :, LANES:2 * LANES] = ones
    _init_softmax(m_ref, l_ref, acc_ref)

    def scores(c, pair, s_ref, mx_ref):
        off = pl.multiple_of(c * t, t)
        kt = jnp.concatenate([k_ref[0, pl.ds(off, t), pair * LANES:(pair + 1) * LANES],
                              f_ref[0, pl.ds(off, t), :]], axis=1)
        for e in range(2):
            st = lax.dot_general(kt, qm_ref[2 * pair + e], _NT, preferred_element_type=F32)
            _scores_out(st, e, s_ref, mx_ref)

    def causal(e, r, block):
        ks = lax.broadcasted_iota(I32, block.shape, 0) + r
        qs = lax.broadcasted_iota(I32, block.shape, 1)
        return jnp.where(ks <= qs, block, NEG)

    def softmax(pair, s_ref, mx_ref, p_ref, a_ref, near):
        _softmax_stage(pair, s_ref, mx_ref, p_ref, a_ref, m_ref, l_ref, causal if near == 0 else None)

    def values(c, pair, p_ref, a_ref):
        vt = vt_ref[c, pair * LANES:(pair + 1) * LANES, :]
        _values_stage(pair, vt, p_ref, a_ref, acc_ref)

    _causal_attention(i, npairs, scores, softmax, values, ((s_a, s_b), (mx_a, mx_b), (p_a, p_b), (a_a, a_b)))
    for pair in range(npairs):
        o_ref[0, :, pair * LANES:(pair + 1) * LANES] = _pair_output(
            pair, l_ref, acc_ref).astype(o_ref.dtype)


def _fox(qf, kf, vft, fk):
    B, S, W = qf.shape
    t = ATT_T
    nk = S // t
    return pl.pallas_call(
        _fox_kernel,
        out_shape=jax.ShapeDtypeStruct((B, S, W), BF16),
        grid=(B, nk),
        in_specs=[pl.BlockSpec((1, t, W), lambda b, i: (b, i, 0)),
                  _resident((1, S, W), lambda b, i: (b, 0, 0)),
                  _resident((nk, W, t), lambda b, i: (b, 0, 0)),
                  _resident((1, S, LANES), lambda b, i: (b, 0, 0))],
        out_specs=pl.BlockSpec((1, t, W), lambda b, i: (b, i, 0)),
        scratch_shapes=[pltpu.VMEM((N_FOX_HEADS, t, 2 * LANES), BF16)]
                       + _attention_scratch(t, N_FOX_HEADS),
        compiler_params=pltpu.CompilerParams(
            dimension_semantics=("arbitrary", "arbitrary"),
            vmem_limit_bytes=VMEM_LIMIT),
        name="fox",
    )(qf, kf, vft, fk)


def _bias_rows(tab_ref, h, t, previous, r, rows):
    nb = t // BIAS_T
    a, r0 = divmod(r, BIAS_T)
    zero = jnp.zeros((rows, BIAS_T), F32)
    blocks = []
    for b in range(nb):
        d = b - a + (nb if previous else 0)
        blocks.append(tab_ref[h, d, r0:r0 + rows, :] if d in (0, 1) else zero)
    return jnp.concatenate(blocks, axis=1)


def _dsa_kernel(qd_ref, qi_ref, w_ref, kd_ref, vt_ref, kk_ref, tab_ref, o_ref,
                sc_ref, half_ref, delta_ref, qm_ref, qim_ref,
                s_a, s_b, mx_a, mx_b, p_a, p_b, a_a, a_b, acc_ref, m_ref, l_ref):
    i = pl.program_id(1)
    t = qd_ref.shape[1]
    S = kd_ref.shape[1]
    topk = float(min(TOPK_MAX, S // 4))
    lane = lax.broadcasted_iota(I32, (t, LANES), 1)
    ks = lax.broadcasted_iota(I32, (t, t), 0)
    qs = lax.broadcasted_iota(I32, (t, t), 1)
    causal = ks <= qs

    for p in range(N_DSA_HEADS // 2):
        a, b = _split_pair(qd_ref[0, :, p * LANES:(p + 1) * LANES], lane)
        qm_ref[2 * p] = a
        qm_ref[2 * p + 1] = b
    for p in range(N_IDX_HEADS // 2):
        a, b = _split_pair(qi_ref[0, :, p * LANES:(p + 1) * LANES], lane)
        qim_ref[2 * p] = a
        qim_ref[2 * p + 1] = b
    wrows = [w_ref[0, h:h + 1, :] * (N_IDX_HEADS ** -0.5) for h in range(N_IDX_HEADS)]

    def score_chunk(c, diag):
        kt = kk_ref[0, pl.ds(pl.multiple_of(c * t, t), t), :]
        sc = None
        for h in range(N_IDX_HEADS):
            d = lax.dot_general(kt, qim_ref[h], _NT, preferred_element_type=F32)
            term = jnp.maximum(d, 0.0) * wrows[h]
            sc = term if sc is None else sc + term
        if diag:
            sc = jnp.where(causal, sc, -jnp.inf)
        sc_ref[c] = sc
        half_ref[c] = sc.astype(BF16)

    def score_body(c, carry):
        score_chunk(c, False)
        return carry

    lax.fori_loop(0, i, score_body, 0)
    score_chunk(i, True)
    nch = i + 1

    def count(pred):
        def body(c, acc):
            v = jnp.where(pred(sc_ref[c]), 1.0, 0.0)
            return acc + jnp.sum(v.reshape(t // SUBLANES, SUBLANES, t), axis=0)
        acc = lax.fori_loop(0, nch, body, jnp.zeros((SUBLANES, t), F32))
        return jnp.sum(acc, axis=0, keepdims=True)

    def ordered_bits_to_float(u):
        return lax.bitcast_convert_type(jnp.where(u < 0, u ^ 0x7FFFFFFF, u), F32)

    def count_half(cand):
        cand_rows = jnp.broadcast_to(cand, (PACKED_ROWS, t))
        one = jnp.ones((PACKED_ROWS, t), BF16)
        zero = jnp.zeros((PACKED_ROWS, t), BF16)

        def body(c, accs):
            accs = list(accs)
            for k in range(t // PACKED_ROWS):
                x = half_ref[c, k * PACKED_ROWS:(k + 1) * PACKED_ROWS, :]
                accs[k % len(accs)] = accs[k % len(accs)] + jnp.where(x >= cand_rows, one, zero)
            return tuple(accs)

        accs = lax.fori_loop(0, nch, body, (zero,) * 4)
        total = sum(a.astype(F32) for a in accs)
        return jnp.sum(total, axis=0, keepdims=True)

    def half_step(jj, prefix):
        cand = prefix + jnp.left_shift(jnp.int32(1), 15 - jj)
        bits = jnp.bitwise_and(jnp.where(cand < 0, cand ^ 0x7FFF, cand), 0xFFFF)
        cand_f = lax.bitcast_convert_type(jnp.left_shift(bits, 16), F32).astype(BF16)
        return jnp.where(count_half(cand_f) >= topk, cand, prefix)

    half = lax.fori_loop(0, 16, half_step, jnp.full((1, t), -HALF_SPAN, I32))

    lo = jnp.maximum(jnp.left_shift(jnp.maximum(half - 1, -HALF_SPAN), 16), NEG_INF_BITS)
    hi = jnp.where(half + 2 >= HALF_SPAN, INT_MAX, jnp.left_shift(half + 2, 16))

    def range_step(_, carry):
        lo, hi = carry
        mid = lo + jnp.right_shift(hi - lo, 1)
        enough = count(lambda x: x >= ordered_bits_to_float(mid)) >= topk
        return jnp.where(enough, mid, lo), jnp.where(enough, hi, mid)

    tau_bits, _ = lax.fori_loop(0, RANGE_STEPS, range_step, (lo, hi))
    tau = ordered_bits_to_float(tau_bits)
    q_pos = i * t + lax.broadcasted_iota(I32, (1, t), 1)
    tau = jnp.where(q_pos + 1 < int(topk), -F32_MAX, tau)

    above = count(lambda x: x - tau > 0.0)
    delta_ref[0:1, :] = jnp.zeros((1, t), F32)
    delta_ref[1:2, :] = topk - above

    @pl.when(jnp.max(jnp.where(above >= topk, 1, 0)) > 0)
    def _():
        gap = ordered_bits_to_float(tau_bits + 1) - tau

        def delta_step(jj, carry):
            delta, step = carry
            cand = delta + step
            cnt = count(lambda x: x - tau >= cand)
            return jnp.where(cnt >= topk, cand, delta), step * 0.5

        delta, _ = lax.fori_loop(0, GAP_STEPS, delta_step, (jnp.zeros((1, t), F32), gap * 0.5))
        delta_ref[0:1, :] = delta
        delta_ref[1:2, :] = topk - count(lambda x: x - tau > delta)

    delta = delta_ref[0:1, :]
    need = delta_ref[1:2, :]

    lower = jnp.where(qs < ks, 1.0, 0.0).astype(BF16)

    def mask_chunk(c, before, diag):
        d = sc_ref[c] - tau
        eq = d == delta
        eqf = jnp.where(eq, 1.0, 0.0)
        rank = jnp.dot(lower, eqf.astype(BF16), preferred_element_type=F32) + before
        tie = jnp.where(rank < need, 0.0, NEG)
        madd = jnp.where(d > delta, 0.0, jnp.where(eq, tie, NEG))
        if diag:
            madd = jnp.where(causal, madd, NEG)
        sc_ref[c] = madd
        return before + jnp.sum(eqf, axis=0, keepdims=True)

    before = lax.fori_loop(0, i, lambda c, b: mask_chunk(c, b, False), jnp.zeros((1, t), F32))
    mask_chunk(i, before, True)

    npairs = N_DSA_HEADS // 2
    _init_softmax(m_ref, l_ref, acc_ref)

    def scores(c, pair, s_ref, mx_ref):
        off = pl.multiple_of(c * t, t)
        kt = kd_ref[0, pl.ds(off, t), pair * LANES:(pair + 1) * LANES]
        madd = sc_ref[c]
        for e in range(2):
            s_ref[e] = lax.dot_general(kt, qm_ref[2 * pair + e], _NT,
                                       preferred_element_type=F32) + madd

    def softmax(pair, s_ref, mx_ref, p_ref, a_ref, near):
        bias = None
        if near is not None:
            bias = lambda e, r, block: block + _bias_rows(
                tab_ref, 2 * pair + e, t, near == 1, r, block.shape[0])
        _softmax_stage(pair, s_ref, None, p_ref, a_ref, m_ref, l_ref, bias)

    def values(c, pair, p_ref, a_ref):
        vt = vt_ref[c, pair * LANES:(pair + 1) * LANES, :]
        _values_stage(pair, vt, p_ref, a_ref, acc_ref)

    _causal_attention(i, npairs, scores, softmax, values, ((s_a, s_b), (mx_a, mx_b), (p_a, p_b), (a_a, a_b)))
    for pair in range(npairs):
        o_ref[0, :, pair * LANES:(pair + 1) * LANES] = _pair_output(
            pair, l_ref, acc_ref).astype(o_ref.dtype)


def _dsa(qd, qi, small, kd, vdt, kk, tab):
    B, S, W = qd.shape
    t = ATT_T
    nk = S // t
    return pl.pallas_call(
        _dsa_kernel,
        out_shape=jax.ShapeDtypeStruct((B, S, W), BF16),
        grid=(B, nk),
        in_specs=[pl.BlockSpec((1, t, W), lambda b, i: (b, i, 0)),
                  pl.BlockSpec((1, t, qi.shape[2]), lambda b, i: (b, i, 0)),
                  pl.BlockSpec((1, SUBLANES, t), lambda b, i: (b, 1, i)),
                  _resident((1, S, W), lambda b, i: (b, 0, 0)),
                  _resident((nk, W, t), lambda b, i: (b, 0, 0)),
                  _resident((1, S, LANES), lambda b, i: (b, 0, 0)),
                  _resident(tab.shape, lambda b, i: (0, 0, 0, 0))],
        out_specs=pl.BlockSpec((1, t, W), lambda b, i: (b, i, 0)),
        scratch_shapes=[pltpu.VMEM((nk, t, t), F32),
                        pltpu.VMEM((nk, t, t), BF16),
                        pltpu.VMEM((2, t), F32),
                        pltpu.VMEM((N_DSA_HEADS, t, LANES), BF16),
                        pltpu.VMEM((N_IDX_HEADS, t, LANES), BF16)]
                       + _attention_scratch(t, N_DSA_HEADS),
        compiler_params=pltpu.CompilerParams(
            dimension_semantics=("arbitrary", "arbitrary"),
            vmem_limit_bytes=VMEM_LIMIT),
        name="dsa",
    )(qd, qi, small, kd, vdt, kk, tab)


def _post_kernel(x_ref, yf_ref, yd_ref, sgf_ref, sgd_ref, ada_ref, g2_ref, gf_ref,
                 wbf_ref, wbd_ref, wo_ref, w1_ref, w2_ref, o_ref):
    bf = jnp.dot(yf_ref[...], wbf_ref[...], preferred_element_type=F32)
    bd = jnp.dot(yd_ref[...], wbd_ref[...], preferred_element_type=F32)
    merged = sgf_ref[...].astype(F32) * bf + sgd_ref[...].astype(F32) * bd
    o = jnp.dot(merged.astype(BF16), wo_ref[...], preferred_element_type=F32)
    x1 = x_ref[...] + ada_ref[0, 2:3, :] * o
    h2 = _rms_mod(x1, g2_ref[...], ada_ref[0, 4:5, :], ada_ref[0, 3:4, :]).astype(BF16)
    acc = jnp.zeros(x1.shape, F32)
    for k in range(w1_ref.shape[1] // FF_CHUNK):
        u = jnp.dot(h2, w1_ref[:, k * FF_CHUNK:(k + 1) * FF_CHUNK], preferred_element_type=F32)
        u = jnp.square(jnp.maximum(u, 0.0)).astype(BF16)
        acc = acc + jnp.dot(u, w2_ref[k * FF_CHUNK:(k + 1) * FF_CHUNK, :],
                            preferred_element_type=F32)
    x2 = x1 + ada_ref[0, 5:6, :] * acc
    y = x2 * lax.rsqrt(jnp.mean(x2 * x2, axis=-1, keepdims=True) + EPS) * gf_ref[...]
    o_ref[...] = y


def _post(x2d, yf, yd, sgf, sgd, ada, g2, gfin, wbf, wbd, wo, w1, w2, S):
    M, D = x2d.shape
    tm = POST_TM
    tpb = S // tm
    rows = lambda w: pl.BlockSpec((tm, w), lambda i: (i, 0))
    full = lambda a: _resident(a.shape, lambda i: (0,) * a.ndim)
    return pl.pallas_call(
        _post_kernel,
        out_shape=jax.ShapeDtypeStruct((M, D), F32),
        grid=(M // tm,),
        in_specs=[rows(D), rows(yf.shape[1]), rows(yd.shape[1]), rows(D), rows(D),
                  pl.BlockSpec((1,) + ada.shape[1:], lambda i: (i // tpb, 0, 0)),
                  pl.BlockSpec((1, D), lambda i: (0, 0)),
                  pl.BlockSpec((1, D), lambda i: (0, 0)),
                  full(wbf), full(wbd), full(wo), full(w1), full(w2)],
        out_specs=rows(D),
        compiler_params=pltpu.CompilerParams(dimension_semantics=("arbitrary",),
                                             vmem_limit_bytes=VMEM_LIMIT),
        name="post",
    )(x2d, yf, yd, sgf, sgd, ada, g2, gfin, wbf, wbd, wo, w1, w2)


def kernel(x, c, w_ada, b_ada, g_norm1, w_in, b_forget, rel_bias, w_branch_fox,
           w_branch_dsa, w_out, g_norm2, w_mlp1, w_mlp2, g_final):
    B, S, D = x.shape
    fw = N_FOX_HEADS * HEAD_DIM
    dw = N_DSA_HEADS * HEAD_DIM
    iw = N_IDX_HEADS * IDX_DIM
    sizes = [fw, fw, fw, N_FOX_HEADS, dw, dw, dw, iw, IDX_DIM, N_IDX_HEADS, D, D]
    offs = [0]
    for s_ in sizes:
        offs.append(offs[-1] + s_)
    seg = lambda k: w_in[:, offs[k]:offs[k + 1]]
    q_scale = HEAD_DIM ** -0.5 * LOG2E
    i_scale = IDX_DIM ** -0.5
    w_main = jnp.concatenate(
        [seg(0) * q_scale, seg(1), seg(4) * q_scale, seg(5),
         seg(7) * i_scale, seg(8), seg(8), seg(10), seg(11)], axis=1).astype(BF16)
    w_t = jnp.concatenate([seg(2), seg(6)], axis=1).T.astype(BF16)
    w_small_t = (jnp.zeros((SMALL_ROWS, D), F32)
                 .at[:N_FOX_HEADS].set(seg(3).T)
                 .at[SUBLANES:SUBLANES + N_IDX_HEADS].set(seg(9).T)).astype(BF16)

    ada = _ada(c, w_ada, b_ada)
    tab = _biastab(rel_bias, BIAS_T)
    x2d = x.reshape(B * S, D)
    (qf, kf, qd, kd, qi, kk, sgf, sgd, vft, vdt, small) = _proj(
        x2d, ada, g_norm1.reshape(1, D), w_main, w_t, w_small_t, B, S)
    fk = _fcum(small, b_forget)
    r3 = lambda a: a.reshape(B, S, a.shape[-1])
    y_fox = _fox(r3(qf), r3(kf), vft, fk)
    y_dsa = _dsa(r3(qd), r3(qi), small, r3(kd), vdt, r3(kk), tab)
    out = _post(x2d, y_fox.reshape(B * S, fw), y_dsa.reshape(B * S, dw), sgf, sgd, ada,
                g_norm2.reshape(1, D), g_final.reshape(1, D),
                w_branch_fox.astype(BF16), w_branch_dsa.astype(BF16), w_out.astype(BF16),
                w_mlp1.astype(BF16), w_mlp2.astype(BF16), S)
    return out.reshape(B, S, D)
```

```python
import math

import jax
import jax.numpy as jnp
import numpy as np
from jax import lax
from jax.experimental import pallas as pl
from jax.experimental.pallas import tpu as pltpu

F32 = jnp.float32
BF16 = jnp.bfloat16
I32 = jnp.int32

HEAD_DIM = 64
N_FOX_HEADS = 8
N_DSA_HEADS = 8
N_IDX_HEADS = 4
IDX_DIM = 64
TOPK_MAX = 256
N_BUCKETS = 32
MAX_DISTANCE = 128
EPS = 1e-6
LOG2E = float(np.float32(math.log2(math.e)))

LANES = 128
SUBLANES = 8
PACKED_ROWS = 16
INT_MIN = -2 ** 31
INT_MAX = 2 ** 31 - 1
HALF_SPAN = 2 ** 15
NEG_INF_BITS = INT_MIN + 0x7FFFFF
RANGE_STEPS = 18
F32_MAX = float(np.finfo(np.float32).max)
NEG = -1e30
VMEM_LIMIT = 52 * 1024 * 1024

PROJ_TM = 512
ATT_T = 512
BIAS_T = MAX_DISTANCE
POST_TM = 512
FF_CHUNK = 1024
CUM_CHUNK = 512
SMALL_ROWS = 16
SOFTMAX_ROWS = 128
F_TERMS = 3
GAP_STEPS = 30

_NT = (((1,), (1,)), ((), ()))


def _resident(block_shape, index_map):
    return pl.BlockSpec(block_shape, index_map, pipeline_mode=pl.Buffered(1))


def _ada_kernel(c_ref, w_ref, b_ref, o_ref):
    o_ref[...] = jnp.dot(c_ref[...], w_ref[...], preferred_element_type=F32,
                         precision=lax.Precision.HIGHEST) + b_ref[...]


def _ada(c, w_ada, b_ada):
    B, D = c.shape
    n = w_ada.shape[1]
    rows = SUBLANES
    c_pad = jnp.zeros((rows, D), F32).at[:B].set(c)
    out = pl.pallas_call(
        _ada_kernel,
        out_shape=jax.ShapeDtypeStruct((rows, n), F32),
        grid=(n // D,),
        in_specs=[pl.BlockSpec((rows, D), lambda j: (0, 0)),
                  pl.BlockSpec((D, D), lambda j: (0, j)),
                  pl.BlockSpec((1, D), lambda j: (0, j))],
        out_specs=pl.BlockSpec((rows, D), lambda j: (0, j)),
        compiler_params=pltpu.CompilerParams(vmem_limit_bytes=VMEM_LIMIT),
        name="ada",
    )(c_pad, w_ada, b_ada.reshape(1, n))
    return out[:B].reshape(B, n // D, D)


def _bucket_tiles(t):
    ks = np.arange(t, dtype=np.int32)[None, :, None]
    qs = np.arange(t, dtype=np.int32)[None, None, :]
    n = np.maximum(qs - ks + np.arange(2, dtype=np.int32)[:, None, None] * t, 0)
    max_exact = N_BUCKETS // 2
    nf = np.maximum(n, 1).astype(np.float32)
    scaled = (np.log(nf / np.float32(max_exact))
              / np.float32(math.log(MAX_DISTANCE / max_exact))
              * np.float32(N_BUCKETS - max_exact)).astype(np.float32)
    large = np.minimum(max_exact + scaled.astype(np.int32), N_BUCKETS - 1)
    return np.where(n < max_exact, n, large).astype(np.int32)


def _biastab_kernel(rb_ref, bucket_ref, o_ref):
    h = pl.program_id(0)
    bucket = bucket_ref[0]
    acc = jnp.zeros(bucket.shape, F32)
    for b in range(N_BUCKETS):
        acc = jnp.where(bucket == b, rb_ref[b, h], acc)
    o_ref[0, 0] = (acc - rb_ref[N_BUCKETS - 1, h]) * LOG2E


def _biastab(rel_bias, t):
    nh = rel_bias.shape[1]
    return pl.pallas_call(
        _biastab_kernel,
        out_shape=jax.ShapeDtypeStruct((nh, 2, t, t), F32),
        grid=(nh, 2),
        in_specs=[pl.BlockSpec(memory_space=pltpu.SMEM),
                  pl.BlockSpec((1, t, t), lambda h, e: (e, 0, 0))],
        out_specs=pl.BlockSpec((1, 1, t, t), lambda h, e: (h, e, 0, 0)),
        name="biastab",
    )(rel_bias, jnp.asarray(_bucket_tiles(t)))


def _rms_mod(x, g, scale, shift):
    y = x * lax.rsqrt(jnp.mean(x * x, axis=-1, keepdims=True) + EPS) * g
    return y * (1.0 + scale) + shift


def _proj_kernel(x_ref, ada_ref, g_ref, w_ref, wt_ref, ws_ref,
                 qf_ref, kf_ref, qd_ref, kd_ref, qi_ref, kk_ref, sgf_ref, sgd_ref,
                 vft_ref, vdt_ref, small_ref):
    h = _rms_mod(x_ref[...], g_ref[...], ada_ref[0, 1:2, :], ada_ref[0, 0:1, :])
    hb = h.astype(BF16)
    col = 0
    for ref in (qf_ref, kf_ref, qd_ref, kd_ref, qi_ref, kk_ref):
        n = ref.shape[-1]
        r = jnp.dot(hb, w_ref[:, col:col + n], preferred_element_type=F32)
        ref[...] = r.astype(ref.dtype)
        col += n
    for ref in (sgf_ref, sgd_ref):
        n = ref.shape[-1]
        r = jnp.dot(hb, w_ref[:, col:col + n], preferred_element_type=F32)
        ref[...] = (1.0 / (1.0 + jnp.exp(-r))).astype(ref.dtype)
        col += n
    row = 0
    for ref in (vft_ref, vdt_ref):
        nt, n, t = ref.shape
        r = lax.dot_general(wt_ref[row:row + n, :], hb, _NT, preferred_element_type=F32)
        for u in range(nt):
            ref[u] = r[:, u * t:(u + 1) * t].astype(ref.dtype)
        row += n
    small_ref[0] = lax.dot_general(ws_ref[...], hb, _NT, preferred_element_type=F32)


def _proj(x2d, ada, g1, w_main, w_t, w_small_t, B, S):
    M, D = x2d.shape
    tm = PROJ_TM
    tpb = S // tm
    t = ATT_T
    fw = N_FOX_HEADS * HEAD_DIM
    iw = N_IDX_HEADS * IDX_DIM
    widths = [fw, fw, fw, fw, iw, LANES, D, D]
    out_shape = [jax.ShapeDtypeStruct((M, w), BF16) for w in widths]
    out_specs = [pl.BlockSpec((tm, w), lambda i: (i, 0)) for w in widths]
    for _ in range(2):
        out_shape.append(jax.ShapeDtypeStruct((M // t, fw, t), BF16))
        out_specs.append(pl.BlockSpec((tm // t, fw, t), lambda i: (i, 0, 0)))
    out_shape.append(jax.ShapeDtypeStruct((B, SMALL_ROWS, S), F32))
    out_specs.append(pl.BlockSpec((1, SMALL_ROWS, tm), lambda i: (i // tpb, 0, i % tpb)))
    return pl.pallas_call(
        _proj_kernel,
        out_shape=out_shape,
        grid=(M // tm,),
        in_specs=[pl.BlockSpec((tm, D), lambda i: (i, 0)),
                  pl.BlockSpec((1,) + ada.shape[1:], lambda i: (i // tpb, 0, 0)),
                  pl.BlockSpec((1, D), lambda i: (0, 0)),
                  _resident(w_main.shape, lambda i: (0, 0)),
                  _resident(w_t.shape, lambda i: (0, 0)),
                  _resident(w_small_t.shape, lambda i: (0, 0))],
        out_specs=out_specs,
        compiler_params=pltpu.CompilerParams(dimension_semantics=("arbitrary",),
                                             vmem_limit_bytes=VMEM_LIMIT),
        name="proj",
    )(x2d, ada, g1, w_main, w_t, w_small_t)


def _fcum_kernel(f_ref, b_ref, o_ref):
    x = f_ref[0] + b_ref[...]
    nh = x.shape[0]
    logf = -(jnp.maximum(-x, 0.0) + jnp.log1p(jnp.exp(-jnp.abs(x))))
    ch = CUM_CHUNK
    r = lax.broadcasted_iota(I32, (ch, ch), 0)
    c = lax.broadcasted_iota(I32, (ch, ch), 1)
    upper = (r <= c).astype(F32)
    carry = jnp.zeros((nh, 1), F32)
    for k in range(x.shape[1] // ch):
        seg = logf[:, k * ch:(k + 1) * ch]
        cs = jnp.dot(seg, upper, preferred_element_type=F32,
                     precision=lax.Precision.HIGHEST) + carry
        carry = cs[:, ch - 1:ch]
        rest = -cs * LOG2E
        terms = []
        for _ in range(F_TERMS):
            term = rest.astype(BF16).astype(F32)
            terms.append(term)
            rest = rest - term
        rows = jnp.concatenate(terms + [jnp.zeros((LANES - F_TERMS * nh, ch), F32)], axis=0)
        o_ref[0, k * ch:(k + 1) * ch, :] = rows.T.astype(BF16)


def _fcum(small, b_forget):
    B, _, S = small.shape
    H = N_FOX_HEADS
    return pl.pallas_call(
        _fcum_kernel,
        out_shape=jax.ShapeDtypeStruct((B, S, LANES), BF16),
        grid=(B,),
        in_specs=[pl.BlockSpec((1, H, S), lambda b: (b, 0, 0)),
                  pl.BlockSpec((H, 1), lambda b: (0, 0))],
        out_specs=pl.BlockSpec((1, S, LANES), lambda b: (b, 0, 0)),
        name="fcum",
    )(small, b_forget.reshape(H, 1))


def _tree(op, xs):
    xs = list(xs)
    while len(xs) > 1:
        xs = [op(xs[k], xs[k + 1]) if k + 1 < len(xs) else xs[k] for k in range(0, len(xs), 2)]
    return xs[0]


def _fold_rows(x, reduce):
    return reduce(x.reshape(x.shape[0] // SUBLANES, SUBLANES, x.shape[1]), axis=0)


def _column_max(block_of):
    t = ATT_T
    parts = [_fold_rows(block_of(r), jnp.max) for r in range(0, t, SOFTMAX_ROWS)]
    return jnp.max(_tree(jnp.maximum, parts), axis=0, keepdims=True)


def _scores_out(st, e, s_ref, mx_ref):
    s_ref[e] = st
    mx_ref[e] = _column_max(lambda r: st[r:r + SOFTMAX_ROWS])


def _softmax_stage(pair, s_ref, mx_ref, p_ref, a_ref, m_ref, l_ref, extra):
    t = s_ref.shape[1]
    rb = SOFTMAX_ROWS
    for e in range(2):
        h = 2 * pair + e
        if extra is not None:
            for r in range(0, t, rb):
                s_ref[e, r:r + rb, :] = extra(e, r, s_ref[e, r:r + rb, :])
        if extra is None and mx_ref is not None:
            m_cur = mx_ref[e]
        else:
            m_cur = _column_max(lambda r: s_ref[e, r:r + rb, :])
        m_prev = m_ref[h]
        m_new = jnp.maximum(m_prev, m_cur)
        alpha = jnp.exp2(m_prev - m_new)
        parts = []
        for r in range(0, t, rb):
            p = jnp.exp2(s_ref[e, r:r + rb, :] - m_new)
            parts.append(_fold_rows(p, jnp.sum))
            p_ref[e, r:r + rb, :] = p.astype(BF16)
        l_ref[h] = alpha * l_ref[h] + jnp.sum(_tree(jnp.add, parts), axis=0, keepdims=True)
        m_ref[h] = m_new
        a_ref[e] = alpha


def _values_stage(pair, vt, p_ref, a_ref, acc_ref):
    for e in range(2):
        h = 2 * pair + e
        ve = vt[e * HEAD_DIM:(e + 1) * HEAD_DIM]
        acc_ref[h] = a_ref[e] * acc_ref[h] + jnp.dot(ve, p_ref[e], preferred_element_type=F32)


def _causal_attention(i, npairs, scores, softmax, values, bufs):
    s_bufs, mx_bufs, p_bufs, a_bufs = bufs
    p_bufs[1][...] = jnp.zeros(p_bufs[1].shape, BF16)
    a_bufs[1][...] = jnp.ones(a_bufs[1].shape, F32)
    scores(0, 0, s_bufs[0], mx_bufs[0])

    def tile_steps(c, near, last_tile):
        for pair in range(npairs):
            cur, oth = pair % 2, 1 - pair % 2
            softmax(pair, s_bufs[cur], mx_bufs[cur], p_bufs[cur], a_bufs[cur], near)
            if pair + 1 < npairs:
                scores(c, pair + 1, s_bufs[oth], mx_bufs[oth])
            elif not last_tile:
                scores(c + 1, 0, s_bufs[oth], mx_bufs[oth])
            if pair >= 1:
                values(c, pair - 1, p_bufs[oth], a_bufs[oth])
            else:
                values(jnp.maximum(c - 1, 0), npairs - 1, p_bufs[oth], a_bufs[oth])

    def far_tile(c, carry):
        tile_steps(c, None, False)
        return carry

    lax.fori_loop(0, jnp.maximum(i - 1, 0), far_tile, 0)

    @pl.when(i >= 1)
    def _():
        tile_steps(i - 1, 1, False)

    tile_steps(i, 0, True)
    last = (npairs - 1) % 2
    values(i, npairs - 1, p_bufs[last], a_bufs[last])


def _attention_scratch(t, nheads):
    pair = lambda dt: pltpu.VMEM((2, t, t), dt)
    row = pltpu.VMEM((2, 1, t), F32)
    return [pair(F32), pair(F32), row, row,
            pair(BF16), pair(BF16), row, row,
            pltpu.VMEM((nheads, HEAD_DIM, t), F32),
            pltpu.VMEM((nheads, 1, t), F32), pltpu.VMEM((nheads, 1, t), F32)]


def _init_softmax(m_ref, l_ref, acc_ref):
    m_ref[...] = jnp.full(m_ref.shape, NEG, F32)
    l_ref[...] = jnp.zeros(l_ref.shape, F32)
    acc_ref[...] = jnp.zeros(acc_ref.shape, F32)


def _pair_output(pair, l_ref, acc_ref):
    h = 2 * pair
    out_t = jnp.concatenate([acc_ref[h] / l_ref[h], acc_ref[h + 1] / l_ref[h + 1]], axis=0)
    return out_t.T


def _split_pair(qp, lane):
    qf = qp.astype(F32)
    return (jnp.where(lane < HEAD_DIM, qf, 0.0).astype(qp.dtype),
            jnp.where(lane >= HEAD_DIM, qf, 0.0).astype(qp.dtype))


def _fox_kernel(q_ref, k_ref, vt_ref, f_ref, o_ref, qm_ref,
                s_a, s_b, mx_a, mx_b, p_a, p_b, a_a, a_b, acc_ref, m_ref, l_ref):
    i = pl.program_id(1)
    t = q_ref.shape[1]
    npairs = N_FOX_HEADS // 2
    lane = lax.broadcasted_iota(I32, (t, LANES), 1)
    for pair in range(npairs):
        halves = _split_pair(q_ref[0, :, pair * LANES:(pair + 1) * LANES], lane)
        for e in range(2):
            h = 2 * pair + e
            ones = jnp.where(lane % N_FOX_HEADS == h, 1.0, 0.0)
            ones = jnp.where(lane < F_TERMS * N_FOX_HEADS, ones, 0.0).astype(BF16)
            qm_ref[h, :, 0:LANES] = halves[e]
            qm_ref[h, :, LANES:2 * LANES] = ones
    _init_softmax(m_ref, l_ref, acc_ref)

    def scores(c, pair, s_ref, mx_ref):
        off = pl.multiple_of(c * t, t)
        kt = jnp.concatenate([k_ref[0, pl.ds(off, t), pair * LANES:(pair + 1) * LANES],
                              f_ref[0, pl.ds(off, t), :]], axis=1)
        for e in range(2):
            st = lax.dot_general(kt, qm_ref[2 * pair + e], _NT, preferred_element_type=F32)
            _scores_out(st, e, s_ref, mx_ref)

    def causal(e, r, block):
        ks = lax.broadcasted_iota(I32, block.shape, 0) + r
        qs = lax.broadcasted_iota(I32, block.shape, 1)
        return jnp.where(ks <= qs, block, NEG)

    def softmax(pair, s_ref, mx_ref, p_ref, a_ref, near):
        _softmax_stage(pair, s_ref, mx_ref, p_ref, a_ref, m_ref, l_ref, causal if near == 0 else None)

    def values(c, pair, p_ref, a_ref):
        vt = vt_ref[c, pair * LANES:(pair + 1) * LANES, :]
        _values_stage(pair, vt, p_ref, a_ref, acc_ref)

    _causal_attention(i, npairs, scores, softmax, values, ((s_a, s_b), (mx_a, mx_b), (p_a, p_b), (a_a, a_b)))
    for pair in range(npairs):
        o_ref[0, :, pair * LANES:(pair + 1) * LANES] = _pair_output(
            pair, l_ref, acc_ref).astype(o_ref.dtype)


def _fox(qf, kf, vft, fk):
    B, S, W = qf.shape
    t = ATT_T
    nk = S // t
    return pl.pallas_call(
        _fox_kernel,
        out_shape=jax.ShapeDtypeStruct((B, S, W), BF16),
        grid=(B, nk),
        in_specs=[pl.BlockSpec((1, t, W), lambda b, i: (b, i, 0)),
                  _resident((1, S, W), lambda b, i: (b, 0, 0)),
                  _resident((nk, W, t), lambda b, i: (b, 0, 0)),
                  _resident((1, S, LANES), lambda b, i: (b, 0, 0))],
        out_specs=pl.BlockSpec((1, t, W), lambda b, i: (b, i, 0)),
        scratch_shapes=[pltpu.VMEM((N_FOX_HEADS, t, 2 * LANES), BF16)]
                       + _attention_scratch(t, N_FOX_HEADS),
        compiler_params=pltpu.CompilerParams(
            dimension_semantics=("arbitrary", "arbitrary"),
            vmem_limit_bytes=VMEM_LIMIT),
        name="fox",
    )(qf, kf, vft, fk)


def _bias_rows(tab_ref, h, t, previous, r, rows):
    nb = t // BIAS_T
    a, r0 = divmod(r, BIAS_T)
    zero = jnp.zeros((rows, BIAS_T), F32)
    blocks = []
    for b in range(nb):
        d = b - a + (nb if previous else 0)
        blocks.append(tab_ref[h, d, r0:r0 + rows, :] if d in (0, 1) else zero)
    return jnp.concatenate(blocks, axis=1)


def _dsa_kernel(qd_ref, qi_ref, w_ref, kd_ref, vt_ref, kk_ref, tab_ref, o_ref,
                sc_ref, half_ref, delta_ref, qm_ref, qim_ref,
                s_a, s_b, mx_a, mx_b, p_a, p_b, a_a, a_b, acc_ref, m_ref, l_ref):
    i = pl.program_id(1)
    t = qd_ref.shape[1]
    S = kd_ref.shape[1]
    topk = float(min(TOPK_MAX, S // 4))
    lane = lax.broadcasted_iota(I32, (t, LANES), 1)
    ks = lax.broadcasted_iota(I32, (t, t), 0)
    qs = lax.broadcasted_iota(I32, (t, t), 1)
    causal = ks <= qs

    for p in range(N_DSA_HEADS // 2):
        a, b = _split_pair(qd_ref[0, :, p * LANES:(p + 1) * LANES], lane)
        qm_ref[2 * p] = a
        qm_ref[2 * p + 1] = b
    for p in range(N_IDX_HEADS // 2):
        a, b = _split_pair(qi_ref[0, :, p * LANES:(p + 1) * LANES], lane)
        qim_ref[2 * p] = a
        qim_ref[2 * p + 1] = b
    wrows = [w_ref[0, h:h + 1, :] * (N_IDX_HEADS ** -0.5) for h in range(N_IDX_HEADS)]

    def score_chunk(c, diag):
        kt = kk_ref[0, pl.ds(pl.multiple_of(c * t, t), t), :]
        sc = None
        for h in range(N_IDX_HEADS):
            d = lax.dot_general(kt, qim_ref[h], _NT, preferred_element_type=F32)
            term = jnp.maximum(d, 0.0) * wrows[h]
            sc = term if sc is None else sc + term
        if diag:
            sc = jnp.where(causal, sc, -jnp.inf)
        sc_ref[c] = sc
        half_ref[c] = sc.astype(BF16)

    def score_body(c, carry):
        score_chunk(c, False)
        return carry

    lax.fori_loop(0, i, score_body, 0)
    score_chunk(i, True)
    nch = i + 1

    def count(pred):
        def body(c, acc):
            v = jnp.where(pred(sc_ref[c]), 1.0, 0.0)
            return acc + jnp.sum(v.reshape(t // SUBLANES, SUBLANES, t), axis=0)
        acc = lax.fori_loop(0, nch, body, jnp.zeros((SUBLANES, t), F32))
        return jnp.sum(acc, axis=0, keepdims=True)

    def ordered_bits_to_float(u):
        return lax.bitcast_convert_type(jnp.where(u < 0, u ^ 0x7FFFFFFF, u), F32)

    def count_half(cand):
        cand_rows = jnp.broadcast_to(cand, (PACKED_ROWS, t))
        one = jnp.ones((PACKED_ROWS, t), BF16)
        zero = jnp.zeros((PACKED_ROWS, t), BF16)

        def body(c, accs):
            accs = list(accs)
            for k in range(t // PACKED_ROWS):
                x = half_ref[c, k * PACKED_ROWS:(k + 1) * PACKED_ROWS, :]
                accs[k % len(accs)] = accs[k % len(accs)] + jnp.where(x >= cand_rows, one, zero)
            return tuple(accs)

        accs = lax.fori_loop(0, nch, body, (zero,) * 4)
        total = sum(a.astype(F32) for a in accs)
        return jnp.sum(total, axis=0, keepdims=True)

    def half_step(jj, prefix):
        cand = prefix + jnp.left_shift(jnp.int32(1), 15 - jj)
        bits = jnp.bitwise_and(jnp.where(cand < 0, cand ^ 0x7FFF, cand), 0xFFFF)
        cand_f = lax.bitcast_convert_type(jnp.left_shift(bits, 16), F32).astype(BF16)
        return jnp.where(count_half(cand_f) >= topk, cand, prefix)

    half = lax.fori_loop(0, 16, half_step, jnp.full((1, t), -HALF_SPAN, I32))

    lo = jnp.maximum(jnp.left_shift(jnp.maximum(half - 1, -HALF_SPAN), 16), NEG_INF_BITS)
    hi = jnp.where(half + 2 >= HALF_SPAN, INT_MAX, jnp.left_shift(half + 2, 16))

    def range_step(_, carry):
        lo, hi = carry
        mid = lo + jnp.right_shift(hi - lo, 1)
        enough = count(lambda x: x >= ordered_bits_to_float(mid)) >= topk
        return jnp.where(enough, mid, lo), jnp.where(enough, hi, mid)

    tau_bits, _ = lax.fori_loop(0, RANGE_STEPS, range_step, (lo, hi))
    tau = ordered_bits_to_float(tau_bits)
    q_pos = i * t + lax.broadcasted_iota(I32, (1, t), 1)
    tau = jnp.where(q_pos + 1 < int(topk), -F32_MAX, tau)

    above = count(lambda x: x > tau)
    delta_ref[0:1, :] = jnp.zeros((1, t), F32)
    delta_ref[1:2, :] = topk - above

    @pl.when(jnp.max(jnp.where(above >= topk, 1, 0)) > 0)
    def _():
        gap = ordered_bits_to_float(tau_bits + 1) - tau

        def delta_step(jj, carry):
            delta, step = carry
            cand = delta + step
            cnt = count(lambda x: x - tau >= cand)
            return jnp.where(cnt >= topk, cand, delta), step * 0.5

        delta, _ = lax.fori_loop(0, GAP_STEPS, delta_step, (jnp.zeros((1, t), F32), gap * 0.5))
        delta_ref[0:1, :] = delta
        delta_ref[1:2, :] = topk - count(lambda x: x - tau > delta)

    delta = delta_ref[0:1, :]
    need = delta_ref[1:2, :]

    lower = jnp.where(qs < ks, 1.0, 0.0).astype(BF16)

    def mask_chunk(c, before, diag):
        d = sc_ref[c] - tau
        eq = d == delta
        eqf = jnp.where(eq, 1.0, 0.0)
        rank = jnp.dot(lower, eqf.astype(BF16), preferred_element_type=F32) + before
        tie = jnp.where(rank < need, 0.0, NEG)
        madd = jnp.where(d > delta, 0.0, jnp.where(eq, tie, NEG))
        if diag:
            madd = jnp.where(causal, madd, NEG)
        sc_ref[c] = madd
        return before + jnp.sum(eqf, axis=0, keepdims=True)

    before = lax.fori_loop(0, i, lambda c, b: mask_chunk(c, b, False), jnp.zeros((1, t), F32))
    mask_chunk(i, before, True)

    npairs = N_DSA_HEADS // 2
    _init_softmax(m_ref, l_ref, acc_ref)

    def scores(c, pair, s_ref, mx_ref):
        off = pl.multiple_of(c * t, t)
        kt = kd_ref[0, pl.ds(off, t), pair * LANES:(pair + 1) * LANES]
        madd = sc_ref[c]
        for e in range(2):
            s_ref[e] = lax.dot_general(kt, qm_ref[2 * pair + e], _NT,
                                       preferred_element_type=F32) + madd

    def softmax(pair, s_ref, mx_ref, p_ref, a_ref, near):
        bias = None
        if near is not None:
            bias = lambda e, r, block: block + _bias_rows(
                tab_ref, 2 * pair + e, t, near == 1, r, block.shape[0])
        _softmax_stage(pair, s_ref, None, p_ref, a_ref, m_ref, l_ref, bias)

    def values(c, pair, p_ref, a_ref):
        vt = vt_ref[c, pair * LANES:(pair + 1) * LANES, :]
        _values_stage(pair, vt, p_ref, a_ref, acc_ref)

    _causal_attention(i, npairs, scores, softmax, values, ((s_a, s_b), (mx_a, mx_b), (p_a, p_b), (a_a, a_b)))
    for pair in range(npairs):
        o_ref[0, :, pair * LANES:(pair + 1) * LANES] = _pair_output(
            pair, l_ref, acc_ref).astype(o_ref.dtype)


def _dsa(qd, qi, small, kd, vdt, kk, tab):
    B, S, W = qd.shape
    t = ATT_T
    nk = S // t
    return pl.pallas_call(
        _dsa_kernel,
        out_shape=jax.ShapeDtypeStruct((B, S, W), BF16),
        grid=(B, nk),
        in_specs=[pl.BlockSpec((1, t, W), lambda b, i: (b, i, 0)),
                  pl.BlockSpec((1, t, qi.shape[2]), lambda b, i: (b, i, 0)),
                  pl.BlockSpec((1, SUBLANES, t), lambda b, i: (b, 1, i)),
                  _resident((1, S, W), lambda b, i: (b, 0, 0)),
                  _resident((nk, W, t), lambda b, i: (b, 0, 0)),
                  _resident((1, S, LANES), lambda b, i: (b, 0, 0)),
                  _resident(tab.shape, lambda b, i: (0, 0, 0, 0))],
        out_specs=pl.BlockSpec((1, t, W), lambda b, i: (b, i, 0)),
        scratch_shapes=[pltpu.VMEM((nk, t, t), F32),
                        pltpu.VMEM((nk, t, t), BF16),
                        pltpu.VMEM((2, t), F32),
                        pltpu.VMEM((N_DSA_HEADS, t, LANES), BF16),
                        pltpu.VMEM((N_IDX_HEADS, t, LANES), BF16)]
                       + _attention_scratch(t, N_DSA_HEADS),
        compiler_params=pltpu.CompilerParams(
            dimension_semantics=("arbitrary", "arbitrary"),
            vmem_limit_bytes=VMEM_LIMIT),
        name="dsa",
    )(qd, qi, small, kd, vdt, kk, tab)


def _post_kernel(x_ref, yf_ref, yd_ref, sgf_ref, sgd_ref, ada_ref, g2_ref, gf_ref,
                 wbf_ref, wbd_ref, wo_ref, w1_ref, w2_ref, o_ref):
    bf = jnp.dot(yf_ref[...], wbf_ref[...], preferred_element_type=F32)
    bd = jnp.dot(yd_ref[...], wbd_ref[...], preferred_element_type=F32)
    merged = sgf_ref[...].astype(F32) * bf + sgd_ref[...].astype(F32) * bd
    o = jnp.dot(merged.astype(BF16), wo_ref[...], preferred_element_type=F32)
    x1 = x_ref[...] + ada_ref[0, 2:3, :] * o
    h2 = _rms_mod(x1, g2_ref[...], ada_ref[0, 4:5, :], ada_ref[0, 3:4, :]).astype(BF16)
    acc = jnp.zeros(x1.shape, F32)
    for k in range(w1_ref.shape[1] // FF_CHUNK):
        u = jnp.dot(h2, w1_ref[:, k * FF_CHUNK:(k + 1) * FF_CHUNK], preferred_element_type=F32)
        u = jnp.square(jnp.maximum(u, 0.0)).astype(BF16)
        acc = acc + jnp.dot(u, w2_ref[k * FF_CHUNK:(k + 1) * FF_CHUNK, :],
                            preferred_element_type=F32)
    x2 = x1 + ada_ref[0, 5:6, :] * acc
    y = x2 * lax.rsqrt(jnp.mean(x2 * x2, axis=-1, keepdims=True) + EPS) * gf_ref[...]
    o_ref[...] = y


def _post(x2d, yf, yd, sgf, sgd, ada, g2, gfin, wbf, wbd, wo, w1, w2, S):
    M, D = x2d.shape
    tm = POST_TM
    tpb = S // tm
    rows = lambda w: pl.BlockSpec((tm, w), lambda i: (i, 0))
    full = lambda a: _resident(a.shape, lambda i: (0,) * a.ndim)
    return pl.pallas_call(
        _post_kernel,
        out_shape=jax.ShapeDtypeStruct((M, D), F32),
        grid=(M // tm,),
        in_specs=[rows(D), rows(yf.shape[1]), rows(yd.shape[1]), rows(D), rows(D),
                  pl.BlockSpec((1,) + ada.shape[1:], lambda i: (i // tpb, 0, 0)),
                  pl.BlockSpec((1, D), lambda i: (0, 0)),
                  pl.BlockSpec((1, D), lambda i: (0, 0)),
                  full(wbf), full(wbd), full(wo), full(w1), full(w2)],
        out_specs=rows(D),
        compiler_params=pltpu.CompilerParams(dimension_semantics=("arbitrary",),
                                             vmem_limit_bytes=VMEM_LIMIT),
        name="post",
    )(x2d, yf, yd, sgf, sgd, ada, g2, gfin, wbf, wbd, wo, w1, w2)


def kernel(x, c, w_ada, b_ada, g_norm1, w_in, b_forget, rel_bias, w_branch_fox,
           w_branch_dsa, w_out, g_norm2, w_mlp1, w_mlp2, g_final):
    B, S, D = x.shape
    fw = N_FOX_HEADS * HEAD_DIM
    dw = N_DSA_HEADS * HEAD_DIM
    iw = N_IDX_HEADS * IDX_DIM
    sizes = [fw, fw, fw, N_FOX_HEADS, dw, dw, dw, iw, IDX_DIM, N_IDX_HEADS, D, D]
    offs = [0]
    for s_ in sizes:
        offs.append(offs[-1] + s_)
    seg = lambda k: w_in[:, offs[k]:offs[k + 1]]
    q_scale = HEAD_DIM ** -0.5 * LOG2E
    i_scale = IDX_DIM ** -0.5
    w_main = jnp.concatenate(
        [seg(0) * q_scale, seg(1), seg(4) * q_scale, seg(5),
         seg(7) * i_scale, seg(8), seg(8), seg(10), seg(11)], axis=1).astype(BF16)
    w_t = jnp.concatenate([seg(2), seg(6)], axis=1).T.astype(BF16)
    w_small_t = (jnp.zeros((SMALL_ROWS, D), F32)
                 .at[:N_FOX_HEADS].set(seg(3).T)
                 .at[SUBLANES:SUBLANES + N_IDX_HEADS].set(seg(9).T)).astype(BF16)

    ada = _ada(c, w_ada, b_ada)
    tab = _biastab(rel_bias, BIAS_T)
    x2d = x.reshape(B * S, D)
    (qf, kf, qd, kd, qi, kk, sgf, sgd, vft, vdt, small) = _proj(
        x2d, ada, g_norm1.reshape(1, D), w_main, w_t, w_small_t, B, S)
    fk = _fcum(small, b_forget)
    r3 = lambda a: a.reshape(B, S, a.shape[-1])
    y_fox = _fox(r3(qf), r3(kf), vft, fk)
    y_dsa = _dsa(r3(qd), r3(qi), small, r3(kd), vdt, r3(kk), tab)
    out = _post(x2d, y_fox.reshape(B * S, fw), y_dsa.reshape(B * S, dw), sgf, sgd, ada,
                g_norm2.reshape(1, D), g_final.reshape(1, D),
                w_branch_fox.astype(BF16), w_branch_dsa.astype(BF16), w_out.astype(BF16),
                w_mlp1.astype(BF16), w_mlp2.astype(BF16), S)
    return out.reshape(B, S, D)
```

```python
import math

import jax
import jax.numpy as jnp
import numpy as np
from jax import lax
from jax.experimental import pallas as pl
from jax.experimental.pallas import tpu as pltpu

F32 = jnp.float32
BF16 = jnp.bfloat16
I32 = jnp.int32

HEAD_DIM = 64
N_FOX_HEADS = 8
N_DSA_HEADS = 8
N_IDX_HEADS = 4
IDX_DIM = 64
TOPK_MAX = 256
N_BUCKETS = 32
MAX_DISTANCE = 128
EPS = 1e-6
LOG2E = float(np.float32(math.log2(math.e)))

LANES = 128
SUBLANES = 8
PACKED_ROWS = 16
INT_MIN = -2 ** 31
INT_MAX = 2 ** 31 - 1
HALF_SPAN = 2 ** 15
NEG_INF_BITS = INT_MIN + 0x7FFFFF
RANGE_STEPS = 18
F32_MAX = float(np.finfo(np.float32).max)
NEG = -1e30
VMEM_LIMIT = 52 * 1024 * 1024

PROJ_TM = 512
ATT_T = 512
BIAS_T = MAX_DISTANCE
POST_TM = 512
FF_CHUNK = 1024
CUM_CHUNK = 512
SMALL_ROWS = 16
SOFTMAX_ROWS = 128
F_TERMS = 3
GAP_STEPS = 30

_NT = (((1,), (1,)), ((), ()))


def _resident(block_shape, index_map):
    return pl.BlockSpec(block_shape, index_map, pipeline_mode=pl.Buffered(1))


def _ada_kernel(c_ref, w_ref, b_ref, o_ref):
    o_ref[...] = jnp.dot(c_ref[...], w_ref[...], preferred_element_type=F32,
                         precision=lax.Precision.HIGHEST) + b_ref[...]


def _ada(c, w_ada, b_ada):
    B, D = c.shape
    n = w_ada.shape[1]
    rows = SUBLANES
    c_pad = jnp.zeros((rows, D), F32).at[:B].set(c)
    out = pl.pallas_call(
        _ada_kernel,
        out_shape=jax.ShapeDtypeStruct((rows, n), F32),
        grid=(n // D,),
        in_specs=[pl.BlockSpec((rows, D), lambda j: (0, 0)),
                  pl.BlockSpec((D, D), lambda j: (0, j)),
                  pl.BlockSpec((1, D), lambda j: (0, j))],
        out_specs=pl.BlockSpec((rows, D), lambda j: (0, j)),
        compiler_params=pltpu.CompilerParams(vmem_limit_bytes=VMEM_LIMIT),
        name="ada",
    )(c_pad, w_ada, b_ada.reshape(1, n))
    return out[:B].reshape(B, n // D, D)


def _bucket_tiles(t):
    ks = np.arange(t, dtype=np.int32)[None, :, None]
    qs = np.arange(t, dtype=np.int32)[None, None, :]
    n = np.maximum(qs - ks + np.arange(2, dtype=np.int32)[:, None, None] * t, 0)
    max_exact = N_BUCKETS // 2
    nf = np.maximum(n, 1).astype(np.float32)
    scaled = (np.log(nf / np.float32(max_exact))
              / np.float32(math.log(MAX_DISTANCE / max_exact))
              * np.float32(N_BUCKETS - max_exact)).astype(np.float32)
    large = np.minimum(max_exact + scaled.astype(np.int32), N_BUCKETS - 1)
    return np.where(n < max_exact, n, large).astype(np.int32)


def _biastab_kernel(rb_ref, bucket_ref, o_ref):
    h = pl.program_id(0)
    bucket = bucket_ref[0]
    acc = jnp.zeros(bucket.shape, F32)
    for b in range(N_BUCKETS):
        acc = jnp.where(bucket == b, rb_ref[b, h], acc)
    o_ref[0, 0] = (acc - rb_ref[N_BUCKETS - 1, h]) * LOG2E


def _biastab(rel_bias, t):
    nh = rel_bias.shape[1]
    return pl.pallas_call(
        _biastab_kernel,
        out_shape=jax.ShapeDtypeStruct((nh, 2, t, t), F32),
        grid=(nh, 2),
        in_specs=[pl.BlockSpec(memory_space=pltpu.SMEM),
                  pl.BlockSpec((1, t, t), lambda h, e: (e, 0, 0))],
        out_specs=pl.BlockSpec((1, 1, t, t), lambda h, e: (h, e, 0, 0)),
        name="biastab",
    )(rel_bias, jnp.asarray(_bucket_tiles(t)))


def _rms_mod(x, g, scale, shift):
    y = x * lax.rsqrt(jnp.mean(x * x, axis=-1, keepdims=True) + EPS) * g
    return y * (1.0 + scale) + shift


def _proj_kernel(x_ref, ada_ref, g_ref, w_ref, wt_ref, ws_ref,
                 qf_ref, kf_ref, qd_ref, kd_ref, qi_ref, kk_ref, sgf_ref, sgd_ref,
                 vft_ref, vdt_ref, small_ref):
    h = _rms_mod(x_ref[...], g_ref[...], ada_ref[0, 1:2, :], ada_ref[0, 0:1, :])
    hb = h.astype(BF16)
    col = 0
    for ref in (qf_ref, kf_ref, qd_ref, kd_ref, qi_ref, kk_ref):
        n = ref.shape[-1]
        r = jnp.dot(hb, w_ref[:, col:col + n], preferred_element_type=F32)
        ref[...] = r.astype(ref.dtype)
        col += n
    for ref in (sgf_ref, sgd_ref):
        n = ref.shape[-1]
        r = jnp.dot(hb, w_ref[:, col:col + n], preferred_element_type=F32)
        ref[...] = (1.0 / (1.0 + jnp.exp(-r))).astype(ref.dtype)
        col += n
    row = 0
    for ref in (vft_ref, vdt_ref):
        nt, n, t = ref.shape
        r = lax.dot_general(wt_ref[row:row + n, :], hb, _NT, preferred_element_type=F32)
        for u in range(nt):
            ref[u] = r[:, u * t:(u + 1) * t].astype(ref.dtype)
        row += n
    small_ref[0] = lax.dot_general(ws_ref[...], hb, _NT, preferred_element_type=F32)


def _proj(x2d, ada, g1, w_main, w_t, w_small_t, B, S):
    M, D = x2d.shape
    tm = PROJ_TM
    tpb = S // tm
    t = ATT_T
    fw = N_FOX_HEADS * HEAD_DIM
    iw = N_IDX_HEADS * IDX_DIM
    widths = [fw, fw, fw, fw, iw, LANES, D, D]
    out_shape = [jax.ShapeDtypeStruct((M, w), BF16) for w in widths]
    out_specs = [pl.BlockSpec((tm, w), lambda i: (i, 0)) for w in widths]
    for _ in range(2):
        out_shape.append(jax.ShapeDtypeStruct((M // t, fw, t), BF16))
        out_specs.append(pl.BlockSpec((tm // t, fw, t), lambda i: (i, 0, 0)))
    out_shape.append(jax.ShapeDtypeStruct((B, SMALL_ROWS, S), F32))
    out_specs.append(pl.BlockSpec((1, SMALL_ROWS, tm), lambda i: (i // tpb, 0, i % tpb)))
    return pl.pallas_call(
        _proj_kernel,
        out_shape=out_shape,
        grid=(M // tm,),
        in_specs=[pl.BlockSpec((tm, D), lambda i: (i, 0)),
                  pl.BlockSpec((1,) + ada.shape[1:], lambda i: (i // tpb, 0, 0)),
                  pl.BlockSpec((1, D), lambda i: (0, 0)),
                  _resident(w_main.shape, lambda i: (0, 0)),
                  _resident(w_t.shape, lambda i: (0, 0)),
                  _resident(w_small_t.shape, lambda i: (0, 0))],
        out_specs=out_specs,
        compiler_params=pltpu.CompilerParams(dimension_semantics=("arbitrary",),
                                             vmem_limit_bytes=VMEM_LIMIT),
        name="proj",
    )(x2d, ada, g1, w_main, w_t, w_small_t)


def _fcum_kernel(f_ref, b_ref, o_ref):
    x = f_ref[0] + b_ref[...]
    nh = x.shape[0]
    logf = -(jnp.maximum(-x, 0.0) + jnp.log1p(jnp.exp(-jnp.abs(x))))
    ch = CUM_CHUNK
    r = lax.broadcasted_iota(I32, (ch, ch), 0)
    c = lax.broadcasted_iota(I32, (ch, ch), 1)
    upper = (r <= c).astype(F32)
    carry = jnp.zeros((nh, 1), F32)
    for k in range(x.shape[1] // ch):
        seg = logf[:, k * ch:(k + 1) * ch]
        cs = jnp.dot(seg, upper, preferred_element_type=F32,
                     precision=lax.Precision.HIGHEST) + carry
        carry = cs[:, ch - 1:ch]
        rest = -cs * LOG2E
        terms = []
        for _ in range(F_TERMS):
            term = rest.astype(BF16).astype(F32)
            terms.append(term)
            rest = rest - term
        rows = jnp.concatenate(terms + [jnp.zeros((LANES - F_TERMS * nh, ch), F32)], axis=0)
        o_ref[0, k * ch:(k + 1) * ch, :] = rows.T.astype(BF16)


def _fcum(small, b_forget):
    B, _, S = small.shape
    H = N_FOX_HEADS
    return pl.pallas_call(
        _fcum_kernel,
        out_shape=jax.ShapeDtypeStruct((B, S, LANES), BF16),
        grid=(B,),
        in_specs=[pl.BlockSpec((1, H, S), lambda b: (b, 0, 0)),
                  pl.BlockSpec((H, 1), lambda b: (0, 0))],
        out_specs=pl.BlockSpec((1, S, LANES), lambda b: (b, 0, 0)),
        name="fcum",
    )(small, b_forget.reshape(H, 1))


def _tree(op, xs):
    xs = list(xs)
    while len(xs) > 1:
        xs = [op(xs[k], xs[k + 1]) if k + 1 < len(xs) else xs[k] for k in range(0, len(xs), 2)]
    return xs[0]


def _fold_rows(x, reduce):
    return reduce(x.reshape(x.shape[0] // SUBLANES, SUBLANES, x.shape[1]), axis=0)


def _column_max(block_of):
    t = ATT_T
    parts = [_fold_rows(block_of(r), jnp.max) for r in range(0, t, SOFTMAX_ROWS)]
    return jnp.max(_tree(jnp.maximum, parts), axis=0, keepdims=True)


def _scores_out(st, e, s_ref, mx_ref):
    s_ref[e] = st
    mx_ref[e] = _column_max(lambda r: st[r:r + SOFTMAX_ROWS])


def _softmax_stage(pair, s_ref, mx_ref, p_ref, a_ref, m_ref, l_ref, extra):
    t = s_ref.shape[1]
    rb = SOFTMAX_ROWS
    for e in range(2):
        h = 2 * pair + e
        if extra is not None:
            for r in range(0, t, rb):
                s_ref[e, r:r + rb, :] = extra(e, r, s_ref[e, r:r + rb, :])
        if extra is None and mx_ref is not None:
            m_cur = mx_ref[e]
        else:
            m_cur = _column_max(lambda r: s_ref[e, r:r + rb, :])
        m_prev = m_ref[h]
        m_new = jnp.maximum(m_prev, m_cur)
        alpha = jnp.exp2(m_prev - m_new)
        parts = []
        for r in range(0, t, rb):
            p = jnp.exp2(s_ref[e, r:r + rb, :] - m_new)
            parts.append(_fold_rows(p, jnp.sum))
            p_ref[e, r:r + rb, :] = p.astype(BF16)
        l_ref[h] = alpha * l_ref[h] + jnp.sum(_tree(jnp.add, parts), axis=0, keepdims=True)
        m_ref[h] = m_new
        a_ref[e] = alpha


def _values_stage(pair, vt, p_ref, a_ref, acc_ref):
    for e in range(2):
        h = 2 * pair + e
        ve = vt[e * HEAD_DIM:(e + 1) * HEAD_DIM]
        acc_ref[h] = a_ref[e] * acc_ref[h] + jnp.dot(ve, p_ref[e], preferred_element_type=F32)


def _causal_attention(i, npairs, scores, softmax, values, bufs):
    s_bufs, mx_bufs, p_bufs, a_bufs = bufs
    p_bufs[1][...] = jnp.zeros(p_bufs[1].shape, BF16)
    a_bufs[1][...] = jnp.ones(a_bufs[1].shape, F32)
    scores(0, 0, s_bufs[0], mx_bufs[0])

    def tile_steps(c, near, last_tile):
        for pair in range(npairs):
            cur, oth = pair % 2, 1 - pair % 2
            softmax(pair, s_bufs[cur], mx_bufs[cur], p_bufs[cur], a_bufs[cur], near)
            if pair + 1 < npairs:
                scores(c, pair + 1, s_bufs[oth], mx_bufs[oth])
            elif not last_tile:
                scores(c + 1, 0, s_bufs[oth], mx_bufs[oth])
            if pair >= 1:
                values(c, pair - 1, p_bufs[oth], a_bufs[oth])
            else:
                values(jnp.maximum(c - 1, 0), npairs - 1, p_bufs[oth], a_bufs[oth])

    def far_tile(c, carry):
        tile_steps(c, None, False)
        return carry

    lax.fori_loop(0, jnp.maximum(i - 1, 0), far_tile, 0)

    @pl.when(i >= 1)
    def _():
        tile_steps(i - 1, 1, False)

    tile_steps(i, 0, True)
    last = (npairs - 1) % 2
    values(i, npairs - 1, p_bufs[last], a_bufs[last])


def _attention_scratch(t, nheads):
    pair = lambda dt: pltpu.VMEM((2, t, t), dt)
    row = pltpu.VMEM((2, 1, t), F32)
    return [pair(F32), pair(F32), row, row,
            pair(BF16), pair(BF16), row, row,
            pltpu.VMEM((nheads, HEAD_DIM, t), F32),
            pltpu.VMEM((nheads, 1, t), F32), pltpu.VMEM((nheads, 1, t), F32)]


def _init_softmax(m_ref, l_ref, acc_ref):
    m_ref[...] = jnp.full(m_ref.shape, NEG, F32)
    l_ref[...] = jnp.zeros(l_ref.shape, F32)
    acc_ref[...] = jnp.zeros(acc_ref.shape, F32)


def _pair_output(pair, l_ref, acc_ref):
    h = 2 * pair
    out_t = jnp.concatenate([acc_ref[h] / l_ref[h], acc_ref[h + 1] / l_ref[h + 1]], axis=0)
    return out_t.T


def _split_pair(qp, lane):
    qf = qp.astype(F32)
    return (jnp.where(lane < HEAD_DIM, qf, 0.0).astype(qp.dtype),
            jnp.where(lane >= HEAD_DIM, qf, 0.0).astype(qp.dtype))


def _fox_kernel(q_ref, k_ref, vt_ref, f_ref, o_ref, qm_ref,
                s_a, s_b, mx_a, mx_b, p_a, p_b, a_a, a_b, acc_ref, m_ref, l_ref):
    i = pl.program_id(1)
    t = q_ref.shape[1]
    npairs = N_FOX_HEADS // 2
    lane = lax.broadcasted_iota(I32, (t, LANES), 1)
    for pair in range(npairs):
        halves = _split_pair(q_ref[0, :, pair * LANES:(pair + 1) * LANES], lane)
        for e in range(2):
            h = 2 * pair + e
            ones = jnp.where(lane % N_FOX_HEADS == h, 1.0, 0.0)
            ones = jnp.where(lane < F_TERMS * N_FOX_HEADS, ones, 0.0).astype(BF16)
            qm_ref[h, :, 0:LANES] = halves[e]
            qm_ref[h, :, LANES:2 * LANES] = ones
    _init_softmax(m_ref, l_ref, acc_ref)

    def scores(c, pair, s_ref, mx_ref):
        off = pl.multiple_of(c * t, t)
        kt = jnp.concatenate([k_ref[0, pl.ds(off, t), pair * LANES:(pair + 1) * LANES],
                              f_ref[0, pl.ds(off, t), :]], axis=1)
        for e in range(2):
            st = lax.dot_general(kt, qm_ref[2 * pair + e], _NT, preferred_element_type=F32)
            _scores_out(st, e, s_ref, mx_ref)

    def causal(e, r, block):
        ks = lax.broadcasted_iota(I32, block.shape, 0) + r
        qs = lax.broadcasted_iota(I32, block.shape, 1)
        return jnp.where(ks <= qs, block, NEG)

    def softmax(pair, s_ref, mx_ref, p_ref, a_ref, near):
        _softmax_stage(pair, s_ref, mx_ref, p_ref, a_ref, m_ref, l_ref, causal if near == 0 else None)

    def values(c, pair, p_ref, a_ref):
        vt = vt_ref[c, pair * LANES:(pair + 1) * LANES, :]
        _values_stage(pair, vt, p_ref, a_ref, acc_ref)

    _causal_attention(i, npairs, scores, softmax, values, ((s_a, s_b), (mx_a, mx_b), (p_a, p_b), (a_a, a_b)))
    for pair in range(npairs):
        o_ref[0, :, pair * LANES:(pair + 1) * LANES] = _pair_output(
            pair, l_ref, acc_ref).astype(o_ref.dtype)


def _fox(qf, kf, vft, fk):
    B, S, W = qf.shape
    t = ATT_T
    nk = S // t
    return pl.pallas_call(
        _fox_kernel,
        out_shape=jax.ShapeDtypeStruct((B, S, W), BF16),
        grid=(B, nk),
        in_specs=[pl.BlockSpec((1, t, W), lambda b, i: (b, i, 0)),
                  _resident((1, S, W), lambda b, i: (b, 0, 0)),
                  _resident((nk, W, t), lambda b, i: (b, 0, 0)),
                  _resident((1, S, LANES), lambda b, i: (b, 0, 0))],
        out_specs=pl.BlockSpec((1, t, W), lambda b, i: (b, i, 0)),
        scratch_shapes=[pltpu.VMEM((N_FOX_HEADS, t, 2 * LANES), BF16)]
                       + _attention_scratch(t, N_FOX_HEADS),
        compiler_params=pltpu.CompilerParams(
            dimension_semantics=("arbitrary", "arbitrary"),
            vmem_limit_bytes=VMEM_LIMIT),
        name="fox",
    )(qf, kf, vft, fk)


def _bias_rows(tab_ref, h, t, previous, r, rows):
    nb = t // BIAS_T
    a, r0 = divmod(r, BIAS_T)
    zero = jnp.zeros((rows, BIAS_T), F32)
    blocks = []
    for b in range(nb):
        d = b - a + (nb if previous else 0)
        blocks.append(tab_ref[h, d, r0:r0 + rows, :] if d in (0, 1) else zero)
    return jnp.concatenate(blocks, axis=1)


def _dsa_kernel(qd_ref, qi_ref, w_ref, kd_ref, vt_ref, kk_ref, tab_ref, o_ref,
                sc_ref, half_ref, delta_ref, qm_ref, qim_ref,
                s_a, s_b, mx_a, mx_b, p_a, p_b, a_a, a_b, acc_ref, m_ref, l_ref):
    i = pl.program_id(1)
    t = qd_ref.shape[1]
    S = kd_ref.shape[1]
    topk = float(min(TOPK_MAX, S // 4))
    lane = lax.broadcasted_iota(I32, (t, LANES), 1)
    ks = lax.broadcasted_iota(I32, (t, t), 0)
    qs = lax.broadcasted_iota(I32, (t, t), 1)
    causal = ks <= qs

    for p in range(N_DSA_HEADS // 2):
        a, b = _split_pair(qd_ref[0, :, p * LANES:(p + 1) * LANES], lane)
        qm_ref[2 * p] = a
        qm_ref[2 * p + 1] = b
    for p in range(N_IDX_HEADS // 2):
        a, b = _split_pair(qi_ref[0, :, p * LANES:(p + 1) * LANES], lane)
        qim_ref[2 * p] = a
        qim_ref[2 * p + 1] = b
    wrows = [w_ref[0, h:h + 1, :] * (N_IDX_HEADS ** -0.5) for h in range(N_IDX_HEADS)]

    def score_chunk(c, diag):
        kt = kk_ref[0, pl.ds(pl.multiple_of(c * t, t), t), :]
        sc = None
        for h in range(N_IDX_HEADS):
            d = lax.dot_general(kt, qim_ref[h], _NT, preferred_element_type=F32)
            term = jnp.maximum(d, 0.0) * wrows[h]
            sc = term if sc is None else sc + term
        if diag:
            sc = jnp.where(causal, sc, -jnp.inf)
        sc_ref[c] = sc
        half_ref[c] = sc.astype(BF16)

    def score_body(c, carry):
        score_chunk(c, False)
        return carry

    lax.fori_loop(0, i, score_body, 0)
    score_chunk(i, True)
    nch = i + 1

    def count(pred):
        def body(c, acc):
            v = jnp.where(pred(sc_ref[c]), 1.0, 0.0)
            return acc + jnp.sum(v.reshape(t // SUBLANES, SUBLANES, t), axis=0)
        acc = lax.fori_loop(0, nch, body, jnp.zeros((SUBLANES, t), F32))
        return jnp.sum(acc, axis=0, keepdims=True)

    def ordered_bits_to_float(u):
        return lax.bitcast_convert_type(jnp.where(u < 0, u ^ 0x7FFFFFFF, u), F32)

    def count_half(cand):
        cand_rows = jnp.broadcast_to(cand, (PACKED_ROWS, t))
        one = jnp.ones((PACKED_ROWS, t), BF16)
        zero = jnp.zeros((PACKED_ROWS, t), BF16)

        def body(c, accs):
            accs = list(accs)
            for k in range(t // PACKED_ROWS):
                x = half_ref[c, k * PACKED_ROWS:(k + 1) * PACKED_ROWS, :]
                accs[k % len(accs)] = accs[k % len(accs)] + jnp.where(x >= cand_rows, one, zero)
            return tuple(accs)

        accs = lax.fori_loop(0, nch, body, (zero,) * 4)
        total = sum(a.astype(F32) for a in accs)
        return jnp.sum(total, axis=0, keepdims=True)

    def half_step(jj, prefix):
        cand = prefix + jnp.left_shift(jnp.int32(1), 15 - jj)
        bits = jnp.bitwise_and(jnp.where(cand < 0, cand ^ 0x7FFF, cand), 0xFFFF)
        cand_f = lax.bitcast_convert_type(jnp.left_shift(bits, 16), F32).astype(BF16)
        return jnp.where(count_half(cand_f) >= topk, cand, prefix)

    half = lax.fori_loop(0, 16, half_step, jnp.full((1, t), -HALF_SPAN, I32))

    lo = jnp.maximum(jnp.left_shift(jnp.maximum(half - 1, -HALF_SPAN), 16), NEG_INF_BITS)
    hi = jnp.where(half + 2 >= HALF_SPAN, INT_MAX, jnp.left_shift(half + 2, 16))

    def range_step(_, carry):
        lo, hi = carry
        mid = lo + jnp.right_shift(hi - lo, 1)
        enough = count(lambda x: x >= ordered_bits_to_float(mid)) >= topk
        return jnp.where(enough, mid, lo), jnp.where(enough, hi, mid)

    tau_bits, _ = lax.fori_loop(0, RANGE_STEPS, range_step, (lo, hi))
    tau = ordered_bits_to_float(tau_bits)
    q_pos = i * t + lax.broadcasted_iota(I32, (1, t), 1)
    tau = jnp.where(q_pos + 1 < int(topk), -F32_MAX, tau)

    above = count(lambda x: x > tau)
    delta_ref[0:1, :] = jnp.zeros((1, t), F32)
    delta_ref[1:2, :] = topk - above
    delta_ref[2:3, :] = count(lambda x: x == tau)

    @pl.when(jnp.max(jnp.where(above >= topk, 1, 0)) > 0)
    def _():
        gap = ordered_bits_to_float(tau_bits + 1) - tau

        def delta_step(jj, carry):
            delta, step = carry
            cand = delta + step
            cnt = count(lambda x: x - tau >= cand)
            return jnp.where(cnt >= topk, cand, delta), step * 0.5

        delta, _ = lax.fori_loop(0, GAP_STEPS, delta_step, (jnp.zeros((1, t), F32), gap * 0.5))
        delta_ref[0:1, :] = delta
        delta_ref[1:2, :] = topk - count(lambda x: x - tau > delta)
        delta_ref[2:3, :] = count(lambda x: x - tau == delta)

    delta = delta_ref[0:1, :]
    need = delta_ref[1:2, :]
    surplus = jnp.max(jnp.where(delta_ref[2:3, :] > need, 1, 0)) > 0

    def keep_all_ties(c, diag):
        madd = jnp.where(sc_ref[c] - tau >= delta, 0.0, NEG)
        if diag:
            madd = jnp.where(causal, madd, NEG)
        sc_ref[c] = madd

    @pl.when(jnp.logical_not(surplus))
    def _():
        def body(c, carry):
            keep_all_ties(c, False)
            return carry

        lax.fori_loop(0, i, body, 0)
        keep_all_ties(i, True)

    lower = jnp.where(qs < ks, 1.0, 0.0).astype(BF16)

    def mask_chunk(c, before, diag):
        d = sc_ref[c] - tau
        eq = d == delta
        eqf = jnp.where(eq, 1.0, 0.0)
        rank = jnp.dot(lower, eqf.astype(BF16), preferred_element_type=F32) + before
        tie = jnp.where(rank < need, 0.0, NEG)
        madd = jnp.where(d > delta, 0.0, jnp.where(eq, tie, NEG))
        if diag:
            madd = jnp.where(causal, madd, NEG)
        sc_ref[c] = madd
        return before + jnp.sum(eqf, axis=0, keepdims=True)

    @pl.when(surplus)
    def _():
        before = lax.fori_loop(0, i, lambda c, b: mask_chunk(c, b, False),
                               jnp.zeros((1, t), F32))
        mask_chunk(i, before, True)

    npairs = N_DSA_HEADS // 2
    _init_softmax(m_ref, l_ref, acc_ref)

    def scores(c, pair, s_ref, mx_ref):
        off = pl.multiple_of(c * t, t)
        kt = kd_ref[0, pl.ds(off, t), pair * LANES:(pair + 1) * LANES]
        madd = sc_ref[c]
        for e in range(2):
            s_ref[e] = lax.dot_general(kt, qm_ref[2 * pair + e], _NT,
                                       preferred_element_type=F32) + madd

    def softmax(pair, s_ref, mx_ref, p_ref, a_ref, near):
        bias = None
        if near is not None:
            bias = lambda e, r, block: block + _bias_rows(
                tab_ref, 2 * pair + e, t, near == 1, r, block.shape[0])
        _softmax_stage(pair, s_ref, None, p_ref, a_ref, m_ref, l_ref, bias)

    def values(c, pair, p_ref, a_ref):
        vt = vt_ref[c, pair * LANES:(pair + 1) * LANES, :]
        _values_stage(pair, vt, p_ref, a_ref, acc_ref)

    _causal_attention(i, npairs, scores, softmax, values, ((s_a, s_b), (mx_a, mx_b), (p_a, p_b), (a_a, a_b)))
    for pair in range(npairs):
        o_ref[0, :, pair * LANES:(pair + 1) * LANES] = _pair_output(
            pair, l_ref, acc_ref).astype(o_ref.dtype)


def _dsa(qd, qi, small, kd, vdt, kk, tab):
    B, S, W = qd.shape
    t = ATT_T
    nk = S // t
    return pl.pallas_call(
        _dsa_kernel,
        out_shape=jax.ShapeDtypeStruct((B, S, W), BF16),
        grid=(B, nk),
        in_specs=[pl.BlockSpec((1, t, W), lambda b, i: (b, i, 0)),
                  pl.BlockSpec((1, t, qi.shape[2]), lambda b, i: (b, i, 0)),
                  pl.BlockSpec((1, SUBLANES, t), lambda b, i: (b, 1, i)),
                  _resident((1, S, W), lambda b, i: (b, 0, 0)),
                  _resident((nk, W, t), lambda b, i: (b, 0, 0)),
                  _resident((1, S, LANES), lambda b, i: (b, 0, 0)),
                  _resident(tab.shape, lambda b, i: (0, 0, 0, 0))],
        out_specs=pl.BlockSpec((1, t, W), lambda b, i: (b, i, 0)),
        scratch_shapes=[pltpu.VMEM((nk, t, t), F32),
                        pltpu.VMEM((nk, t, t), BF16),
                        pltpu.VMEM((3, t), F32),
                        pltpu.VMEM((N_DSA_HEADS, t, LANES), BF16),
                        pltpu.VMEM((N_IDX_HEADS, t, LANES), BF16)]
                       + _attention_scratch(t, N_DSA_HEADS),
        compiler_params=pltpu.CompilerParams(
            dimension_semantics=("arbitrary", "arbitrary"),
            vmem_limit_bytes=VMEM_LIMIT),
        name="dsa",
    )(qd, qi, small, kd, vdt, kk, tab)


def _post_kernel(x_ref, yf_ref, yd_ref, sgf_ref, sgd_ref, ada_ref, g2_ref, gf_ref,
                 wbf_ref, wbd_ref, wo_ref, w1_ref, w2_ref, o_ref):
    bf = jnp.dot(yf_ref[...], wbf_ref[...], preferred_element_type=F32)
    bd = jnp.dot(yd_ref[...], wbd_ref[...], preferred_element_type=F32)
    merged = sgf_ref[...].astype(F32) * bf + sgd_ref[...].astype(F32) * bd
    o = jnp.dot(merged.astype(BF16), wo_ref[...], preferred_element_type=F32)
    x1 = x_ref[...] + ada_ref[0, 2:3, :] * o
    h2 = _rms_mod(x1, g2_ref[...], ada_ref[0, 4:5, :], ada_ref[0, 3:4, :]).astype(BF16)
    acc = jnp.zeros(x1.shape, F32)
    for k in range(w1_ref.shape[1] // FF_CHUNK):
        u = jnp.dot(h2, w1_ref[:, k * FF_CHUNK:(k + 1) * FF_CHUNK], preferred_element_type=F32)
        u = jnp.square(jnp.maximum(u, 0.0)).astype(BF16)
        acc = acc + jnp.dot(u, w2_ref[k * FF_CHUNK:(k + 1) * FF_CHUNK, :],
                            preferred_element_type=F32)
    x2 = x1 + ada_ref[0, 5:6, :] * acc
    y = x2 * lax.rsqrt(jnp.mean(x2 * x2, axis=-1, keepdims=True) + EPS) * gf_ref[...]
    o_ref[...] = y


def _post(x2d, yf, yd, sgf, sgd, ada, g2, gfin, wbf, wbd, wo, w1, w2, S):
    M, D = x2d.shape
    tm = POST_TM
    tpb = S // tm
    rows = lambda w: pl.BlockSpec((tm, w), lambda i: (i, 0))
    full = lambda a: _resident(a.shape, lambda i: (0,) * a.ndim)
    return pl.pallas_call(
        _post_kernel,
        out_shape=jax.ShapeDtypeStruct((M, D), F32),
        grid=(M // tm,),
        in_specs=[rows(D), rows(yf.shape[1]), rows(yd.shape[1]), rows(D), rows(D),
                  pl.BlockSpec((1,) + ada.shape[1:], lambda i: (i // tpb, 0, 0)),
                  pl.BlockSpec((1, D), lambda i: (0, 0)),
                  pl.BlockSpec((1, D), lambda i: (0, 0)),
                  full(wbf), full(wbd), full(wo), full(w1), full(w2)],
        out_specs=rows(D),
        compiler_params=pltpu.CompilerParams(dimension_semantics=("arbitrary",),
                                             vmem_limit_bytes=VMEM_LIMIT),
        name="post",
    )(x2d, yf, yd, sgf, sgd, ada, g2, gfin, wbf, wbd, wo, w1, w2)


def kernel(x, c, w_ada, b_ada, g_norm1, w_in, b_forget, rel_bias, w_branch_fox,
           w_branch_dsa, w_out, g_norm2, w_mlp1, w_mlp2, g_final):
    B, S, D = x.shape
    fw = N_FOX_HEADS * HEAD_DIM
    dw = N_DSA_HEADS * HEAD_DIM
    iw = N_IDX_HEADS * IDX_DIM
    sizes = [fw, fw, fw, N_FOX_HEADS, dw, dw, dw, iw, IDX_DIM, N_IDX_HEADS, D, D]
    offs = [0]
    for s_ in sizes:
        offs.append(offs[-1] + s_)
    seg = lambda k: w_in[:, offs[k]:offs[k + 1]]
    q_scale = HEAD_DIM ** -0.5 * LOG2E
    i_scale = IDX_DIM ** -0.5
    w_main = jnp.concatenate(
        [seg(0) * q_scale, seg(1), seg(4) * q_scale, seg(5),
         seg(7) * i_scale, seg(8), seg(8), seg(10), seg(11)], axis=1).astype(BF16)
    w_t = jnp.concatenate([seg(2), seg(6)], axis=1).T.astype(BF16)
    w_small_t = (jnp.zeros((SMALL_ROWS, D), F32)
                 .at[:N_FOX_HEADS].set(seg(3).T)
                 .at[SUBLANES:SUBLANES + N_IDX_HEADS].set(seg(9).T)).astype(BF16)

    ada = _ada(c, w_ada, b_ada)
    tab = _biastab(rel_bias, BIAS_T)
    x2d = x.reshape(B * S, D)
    (qf, kf, qd, kd, qi, kk, sgf, sgd, vft, vdt, small) = _proj(
        x2d, ada, g_norm1.reshape(1, D), w_main, w_t, w_small_t, B, S)
    fk = _fcum(small, b_forget)
    r3 = lambda a: a.reshape(B, S, a.shape[-1])
    y_fox = _fox(r3(qf), r3(kf), vft, fk)
    y_dsa = _dsa(r3(qd), r3(qi), small, r3(kd), vdt, r3(kk), tab)
    out = _post(x2d, y_fox.reshape(B * S, fw), y_dsa.reshape(B * S, dw), sgf, sgd, ada,
                g_norm2.reshape(1, D), g_final.reshape(1, D),
                w_branch_fox.astype(BF16), w_branch_dsa.astype(BF16), w_out.astype(BF16),
                w_mlp1.astype(BF16), w_mlp2.astype(BF16), S)
    return out.reshape(B, S, D)
```

```python
import math

import jax
import jax.numpy as jnp
import numpy as np
from jax import lax
from jax.experimental import pallas as pl
from jax.experimental.pallas import tpu as pltpu

F32 = jnp.float32
BF16 = jnp.bfloat16
I32 = jnp.int32

HEAD_DIM = 64
N_FOX_HEADS = 8
N_DSA_HEADS = 8
N_IDX_HEADS = 4
IDX_DIM = 64
TOPK_MAX = 256
N_BUCKETS = 32
MAX_DISTANCE = 128
EPS = 1e-6
LOG2E = float(np.float32(math.log2(math.e)))

LANES = 128
SUBLANES = 8
PACKED_ROWS = 16
INT_MIN = -2 ** 31
INT_MAX = 2 ** 31 - 1
HALF_SPAN = 2 ** 15
NEG_INF_BITS = INT_MIN + 0x7FFFFF
RANGE_STEPS = 18
F32_MAX = float(np.finfo(np.float32).max)
NEG = -1e30
VMEM_LIMIT = 52 * 1024 * 1024

ADA_COLS = 2048
PROJ_TM = 512
ATT_T = 512
BIAS_T = MAX_DISTANCE
POST_TM = 512
FF_CHUNK = 1024
CUM_CHUNK = 256
SMALL_ROWS = 16
SOFTMAX_ROWS = 128
F_TERMS = 3
GAP_STEPS = 30

_NT = (((1,), (1,)), ((), ()))


def _resident(block_shape, index_map):
    return pl.BlockSpec(block_shape, index_map, pipeline_mode=pl.Buffered(1))


def _ada_kernel(c_ref, w_ref, b_ref, o_ref):
    o_ref[...] = jnp.dot(c_ref[...], w_ref[...], preferred_element_type=F32,
                         precision=lax.Precision.HIGHEST) + b_ref[...]


def _ada(c, w_ada, b_ada):
    B, D = c.shape
    n = w_ada.shape[1]
    rows = SUBLANES
    c_pad = jnp.zeros((rows, D), F32).at[:B].set(c)
    out = pl.pallas_call(
        _ada_kernel,
        out_shape=jax.ShapeDtypeStruct((rows, n), F32),
        grid=(n // ADA_COLS,),
        in_specs=[pl.BlockSpec((rows, D), lambda j: (0, 0)),
                  pl.BlockSpec((D, ADA_COLS), lambda j: (0, j)),
                  pl.BlockSpec((1, ADA_COLS), lambda j: (0, j))],
        out_specs=pl.BlockSpec((rows, ADA_COLS), lambda j: (0, j)),
        compiler_params=pltpu.CompilerParams(vmem_limit_bytes=VMEM_LIMIT),
        name="ada",
    )(c_pad, w_ada, b_ada.reshape(1, n))
    return out[:B].reshape(B, n // D, D)


def _bucket_tiles(t):
    ks = np.arange(t, dtype=np.int32)[None, :, None]
    qs = np.arange(t, dtype=np.int32)[None, None, :]
    n = np.maximum(qs - ks + np.arange(2, dtype=np.int32)[:, None, None] * t, 0)
    max_exact = N_BUCKETS // 2
    nf = np.maximum(n, 1).astype(np.float32)
    scaled = (np.log(nf / np.float32(max_exact))
              / np.float32(math.log(MAX_DISTANCE / max_exact))
              * np.float32(N_BUCKETS - max_exact)).astype(np.float32)
    large = np.minimum(max_exact + scaled.astype(np.int32), N_BUCKETS - 1)
    return np.where(n < max_exact, n, large).astype(np.int32)


def _biastab_kernel(rb_ref, bucket_ref, o_ref):
    bucket = bucket_ref[0]
    for h in range(o_ref.shape[0]):
        acc = jnp.zeros(bucket.shape, F32)
        for b in range(N_BUCKETS):
            acc = jnp.where(bucket == b, rb_ref[b, h], acc)
        o_ref[h, 0] = (acc - rb_ref[N_BUCKETS - 1, h]) * LOG2E


def _biastab(rel_bias, t):
    nh = rel_bias.shape[1]
    return pl.pallas_call(
        _biastab_kernel,
        out_shape=jax.ShapeDtypeStruct((nh, 2, t, t), F32),
        grid=(2,),
        in_specs=[pl.BlockSpec(memory_space=pltpu.SMEM),
                  pl.BlockSpec((1, t, t), lambda e: (e, 0, 0))],
        out_specs=pl.BlockSpec((nh, 1, t, t), lambda e: (0, e, 0, 0)),
        name="biastab",
    )(rel_bias, jnp.asarray(_bucket_tiles(t)))


def _rms_mod(x, g, scale, shift):
    y = x * lax.rsqrt(jnp.mean(x * x, axis=-1, keepdims=True) + EPS) * g
    return y * (1.0 + scale) + shift


def _proj_kernel(x_ref, ada_ref, g_ref, w_ref, wt_ref, ws_ref,
                 qf_ref, kf_ref, qd_ref, kd_ref, qi_ref, kk_ref, sgf_ref, sgd_ref,
                 vft_ref, vdt_ref, small_ref):
    h = _rms_mod(x_ref[...], g_ref[...], ada_ref[0, 1:2, :], ada_ref[0, 0:1, :])
    hb = h.astype(BF16)
    col = 0
    for ref in (qf_ref, kf_ref, qd_ref, kd_ref, qi_ref, kk_ref):
        n = ref.shape[-1]
        r = jnp.dot(hb, w_ref[:, col:col + n], preferred_element_type=F32)
        ref[...] = r.astype(ref.dtype)
        col += n
    for ref in (sgf_ref, sgd_ref):
        n = ref.shape[-1]
        r = jnp.dot(hb, w_ref[:, col:col + n], preferred_element_type=F32)
        ref[...] = (1.0 / (1.0 + jnp.exp(-r))).astype(ref.dtype)
        col += n
    row = 0
    for ref in (vft_ref, vdt_ref):
        nt, n, t = ref.shape
        r = lax.dot_general(wt_ref[row:row + n, :], hb, _NT, preferred_element_type=F32)
        for u in range(nt):
            ref[u] = r[:, u * t:(u + 1) * t].astype(ref.dtype)
        row += n
    small_ref[0] = lax.dot_general(ws_ref[...], hb, _NT, preferred_element_type=F32)


def _proj(x2d, ada, g1, w_main, w_t, w_small_t, B, S):
    M, D = x2d.shape
    tm = PROJ_TM
    tpb = S // tm
    t = ATT_T
    fw = N_FOX_HEADS * HEAD_DIM
    iw = N_IDX_HEADS * IDX_DIM
    widths = [fw, fw, fw, fw, iw, LANES, D, D]
    out_shape = [jax.ShapeDtypeStruct((M, w), BF16) for w in widths]
    out_specs = [pl.BlockSpec((tm, w), lambda i: (i, 0)) for w in widths]
    for _ in range(2):
        out_shape.append(jax.ShapeDtypeStruct((M // t, fw, t), BF16))
        out_specs.append(pl.BlockSpec((tm // t, fw, t), lambda i: (i, 0, 0)))
    out_shape.append(jax.ShapeDtypeStruct((B, SMALL_ROWS, S), F32))
    out_specs.append(pl.BlockSpec((1, SMALL_ROWS, tm), lambda i: (i // tpb, 0, i % tpb)))
    return pl.pallas_call(
        _proj_kernel,
        out_shape=out_shape,
        grid=(M // tm,),
        in_specs=[pl.BlockSpec((tm, D), lambda i: (i, 0)),
                  pl.BlockSpec((1,) + ada.shape[1:], lambda i: (i // tpb, 0, 0)),
                  pl.BlockSpec((1, D), lambda i: (0, 0)),
                  _resident(w_main.shape, lambda i: (0, 0)),
                  _resident(w_t.shape, lambda i: (0, 0)),
                  _resident(w_small_t.shape, lambda i: (0, 0))],
        out_specs=out_specs,
        compiler_params=pltpu.CompilerParams(dimension_semantics=("arbitrary",),
                                             vmem_limit_bytes=VMEM_LIMIT),
        name="proj",
    )(x2d, ada, g1, w_main, w_t, w_small_t)


def _fcum_kernel(f_ref, b_ref, o_ref):
    x = f_ref[0] + b_ref[...]
    nh = x.shape[0]
    logf = -(jnp.maximum(-x, 0.0) + jnp.log1p(jnp.exp(-jnp.abs(x))))
    ch = CUM_CHUNK
    r = lax.broadcasted_iota(I32, (ch, ch), 0)
    c = lax.broadcasted_iota(I32, (ch, ch), 1)
    upper = (r <= c).astype(F32)
    carry = jnp.zeros((nh, 1), F32)
    for k in range(x.shape[1] // ch):
        seg = logf[:, k * ch:(k + 1) * ch]
        cs = jnp.dot(seg, upper, preferred_element_type=F32,
                     precision=lax.Precision.HIGHEST) + carry
        carry = cs[:, ch - 1:ch]
        rest = -cs * LOG2E
        terms = []
        for _ in range(F_TERMS):
            term = rest.astype(BF16).astype(F32)
            terms.append(term)
            rest = rest - term
        rows = jnp.concatenate(terms + [jnp.zeros((LANES - F_TERMS * nh, ch), F32)], axis=0)
        o_ref[0, k * ch:(k + 1) * ch, :] = rows.T.astype(BF16)


def _fcum(small, b_forget):
    B, _, S = small.shape
    H = N_FOX_HEADS
    return pl.pallas_call(
        _fcum_kernel,
        out_shape=jax.ShapeDtypeStruct((B, S, LANES), BF16),
        grid=(B,),
        in_specs=[pl.BlockSpec((1, H, S), lambda b: (b, 0, 0)),
                  pl.BlockSpec((H, 1), lambda b: (0, 0))],
        out_specs=pl.BlockSpec((1, S, LANES), lambda b: (b, 0, 0)),
        name="fcum",
    )(small, b_forget.reshape(H, 1))


def _tree(op, xs):
    xs = list(xs)
    while len(xs) > 1:
        xs = [op(xs[k], xs[k + 1]) if k + 1 < len(xs) else xs[k] for k in range(0, len(xs), 2)]
    return xs[0]


def _fold_rows(x, reduce):
    return reduce(x.reshape(x.shape[0] // SUBLANES, SUBLANES, x.shape[1]), axis=0)


def _column_max(block_of):
    t = ATT_T
    parts = [_fold_rows(block_of(r), jnp.max) for r in range(0, t, SOFTMAX_ROWS)]
    return jnp.max(_tree(jnp.maximum, parts), axis=0, keepdims=True)


def _scores_out(st, e, s_ref, mx_ref):
    s_ref[e] = st
    mx_ref[e] = _column_max(lambda r: st[r:r + SOFTMAX_ROWS])


def _softmax_stage(pair, s_ref, mx_ref, p_ref, a_ref, m_ref, l_ref, extra):
    t = s_ref.shape[1]
    rb = SOFTMAX_ROWS
    for e in range(2):
        h = 2 * pair + e
        if extra is not None:
            for r in range(0, t, rb):
                s_ref[e, r:r + rb, :] = extra(e, r, s_ref[e, r:r + rb, :])
        if extra is None and mx_ref is not None:
            m_cur = mx_ref[e]
        else:
            m_cur = _column_max(lambda r: s_ref[e, r:r + rb, :])
        m_prev = m_ref[h]
        m_new = jnp.maximum(m_prev, m_cur)
        alpha = jnp.exp2(m_prev - m_new)
        parts = []
        for r in range(0, t, rb):
            p = jnp.exp2(s_ref[e, r:r + rb, :] - m_new)
            parts.append(_fold_rows(p, jnp.sum))
            p_ref[e, r:r + rb, :] = p.astype(BF16)
        l_ref[h] = alpha * l_ref[h] + jnp.sum(_tree(jnp.add, parts), axis=0, keepdims=True)
        m_ref[h] = m_new
        a_ref[e] = alpha


def _values_stage(pair, vt, p_ref, a_ref, acc_ref):
    for e in range(2):
        h = 2 * pair + e
        ve = vt[e * HEAD_DIM:(e + 1) * HEAD_DIM]
        acc_ref[h] = a_ref[e] * acc_ref[h] + jnp.dot(ve, p_ref[e], preferred_element_type=F32)


def _causal_attention(i, npairs, scores, softmax, values, bufs):
    s_bufs, mx_bufs, p_bufs, a_bufs = bufs
    p_bufs[1][...] = jnp.zeros(p_bufs[1].shape, BF16)
    a_bufs[1][...] = jnp.ones(a_bufs[1].shape, F32)
    scores(0, 0, s_bufs[0], mx_bufs[0])

    def tile_steps(c, near, last_tile):
        for pair in range(npairs):
            cur, oth = pair % 2, 1 - pair % 2
            softmax(pair, s_bufs[cur], mx_bufs[cur], p_bufs[cur], a_bufs[cur], near)
            if pair + 1 < npairs:
                scores(c, pair + 1, s_bufs[oth], mx_bufs[oth])
            elif not last_tile:
                scores(c + 1, 0, s_bufs[oth], mx_bufs[oth])
            if pair >= 1:
                values(c, pair - 1, p_bufs[oth], a_bufs[oth])
            else:
                values(jnp.maximum(c - 1, 0), npairs - 1, p_bufs[oth], a_bufs[oth])

    def far_tile(c, carry):
        tile_steps(c, None, False)
        return carry

    lax.fori_loop(0, jnp.maximum(i - 1, 0), far_tile, 0)

    @pl.when(i >= 1)
    def _():
        tile_steps(i - 1, 1, False)

    tile_steps(i, 0, True)
    last = (npairs - 1) % 2
    values(i, npairs - 1, p_bufs[last], a_bufs[last])


def _attention_scratch(t, nheads):
    pair = lambda dt: pltpu.VMEM((2, t, t), dt)
    row = pltpu.VMEM((2, 1, t), F32)
    return [pair(F32), pair(F32), row, row,
            pair(BF16), pair(BF16), row, row,
            pltpu.VMEM((nheads, HEAD_DIM, t), F32),
            pltpu.VMEM((nheads, 1, t), F32), pltpu.VMEM((nheads, 1, t), F32)]


def _init_softmax(m_ref, l_ref, acc_ref):
    m_ref[...] = jnp.full(m_ref.shape, NEG, F32)
    l_ref[...] = jnp.zeros(l_ref.shape, F32)
    acc_ref[...] = jnp.zeros(acc_ref.shape, F32)


def _pair_output(pair, l_ref, acc_ref):
    h = 2 * pair
    out_t = jnp.concatenate([acc_ref[h] / l_ref[h], acc_ref[h + 1] / l_ref[h + 1]], axis=0)
    return out_t.T


def _split_pair(qp, lane):
    qf = qp.astype(F32)
    return (jnp.where(lane < HEAD_DIM, qf, 0.0).astype(qp.dtype),
            jnp.where(lane >= HEAD_DIM, qf, 0.0).astype(qp.dtype))


def _fox_kernel(q_ref, k_ref, vt_ref, f_ref, o_ref, qm_ref,
                s_a, s_b, mx_a, mx_b, p_a, p_b, a_a, a_b, acc_ref, m_ref, l_ref):
    i = pl.program_id(1)
    t = q_ref.shape[1]
    npairs = N_FOX_HEADS // 2
    lane = lax.broadcasted_iota(I32, (t, LANES), 1)
    for pair in range(npairs):
        halves = _split_pair(q_ref[0, :, pair * LANES:(pair + 1) * LANES], lane)
        for e in range(2):
            h = 2 * pair + e
            ones = jnp.where(lane % N_FOX_HEADS == h, 1.0, 0.0)
            ones = jnp.where(lane < F_TERMS * N_FOX_HEADS, ones, 0.0).astype(BF16)
            qm_ref[h, :, 0:LANES] = halves[e]
            qm_ref[h, :, LANES:2 * LANES] = ones
    _init_softmax(m_ref, l_ref, acc_ref)

    def scores(c, pair, s_ref, mx_ref):
        off = pl.multiple_of(c * t, t)
        kt = jnp.concatenate([k_ref[0, pl.ds(off, t), pair * LANES:(pair + 1) * LANES],
                              f_ref[0, pl.ds(off, t), :]], axis=1)
        for e in range(2):
            st = lax.dot_general(kt, qm_ref[2 * pair + e], _NT, preferred_element_type=F32)
            _scores_out(st, e, s_ref, mx_ref)

    def causal(e, r, block):
        ks = lax.broadcasted_iota(I32, block.shape, 0) + r
        qs = lax.broadcasted_iota(I32, block.shape, 1)
        return jnp.where(ks <= qs, block, NEG)

    def softmax(pair, s_ref, mx_ref, p_ref, a_ref, near):
        _softmax_stage(pair, s_ref, mx_ref, p_ref, a_ref, m_ref, l_ref, causal if near == 0 else None)

    def values(c, pair, p_ref, a_ref):
        vt = vt_ref[c, pair * LANES:(pair + 1) * LANES, :]
        _values_stage(pair, vt, p_ref, a_ref, acc_ref)

    _causal_attention(i, npairs, scores, softmax, values, ((s_a, s_b), (mx_a, mx_b), (p_a, p_b), (a_a, a_b)))
    for pair in range(npairs):
        o_ref[0, :, pair * LANES:(pair + 1) * LANES] = _pair_output(
            pair, l_ref, acc_ref).astype(o_ref.dtype)


def _fox(qf, kf, vft, fk):
    B, S, W = qf.shape
    t = ATT_T
    nk = S // t
    return pl.pallas_call(
        _fox_kernel,
        out_shape=jax.ShapeDtypeStruct((B, S, W), BF16),
        grid=(B, nk),
        in_specs=[pl.BlockSpec((1, t, W), lambda b, i: (b, i, 0)),
                  _resident((1, S, W), lambda b, i: (b, 0, 0)),
                  _resident((nk, W, t), lambda b, i: (b, 0, 0)),
                  _resident((1, S, LANES), lambda b, i: (b, 0, 0))],
        out_specs=pl.BlockSpec((1, t, W), lambda b, i: (b, i, 0)),
        scratch_shapes=[pltpu.VMEM((N_FOX_HEADS, t, 2 * LANES), BF16)]
                       + _attention_scratch(t, N_FOX_HEADS),
        compiler_params=pltpu.CompilerParams(
            dimension_semantics=("arbitrary", "arbitrary"),
            vmem_limit_bytes=VMEM_LIMIT),
        name="fox",
    )(qf, kf, vft, fk)


def _bias_rows(tab_ref, h, t, previous, r, rows):
    nb = t // BIAS_T
    a, r0 = divmod(r, BIAS_T)
    zero = jnp.zeros((rows, BIAS_T), F32)
    blocks = []
    for b in range(nb):
        d = b - a + (nb if previous else 0)
        blocks.append(tab_ref[h, d, r0:r0 + rows, :] if d in (0, 1) else zero)
    return jnp.concatenate(blocks, axis=1)


def _dsa_kernel(qd_ref, qi_ref, w_ref, kd_ref, vt_ref, kk_ref, tab_ref, o_ref,
                sc_ref, half_ref, delta_ref, qm_ref, qim_ref,
                s_a, s_b, mx_a, mx_b, p_a, p_b, a_a, a_b, acc_ref, m_ref, l_ref):
    i = pl.program_id(1)
    t = qd_ref.shape[1]
    S = kd_ref.shape[1]
    topk = float(min(TOPK_MAX, S // 4))
    lane = lax.broadcasted_iota(I32, (t, LANES), 1)
    ks = lax.broadcasted_iota(I32, (t, t), 0)
    qs = lax.broadcasted_iota(I32, (t, t), 1)
    causal = ks <= qs

    for p in range(N_DSA_HEADS // 2):
        a, b = _split_pair(qd_ref[0, :, p * LANES:(p + 1) * LANES], lane)
        qm_ref[2 * p] = a
        qm_ref[2 * p + 1] = b
    for p in range(N_IDX_HEADS // 2):
        a, b = _split_pair(qi_ref[0, :, p * LANES:(p + 1) * LANES], lane)
        qim_ref[2 * p] = a
        qim_ref[2 * p + 1] = b
    wrows = [w_ref[0, h:h + 1, :] * (N_IDX_HEADS ** -0.5) for h in range(N_IDX_HEADS)]

    def score_chunk(c, diag):
        kt = kk_ref[0, pl.ds(pl.multiple_of(c * t, t), t), :]
        sc = None
        for h in range(N_IDX_HEADS):
            d = lax.dot_general(kt, qim_ref[h], _NT, preferred_element_type=F32)
            term = jnp.maximum(d, 0.0) * wrows[h]
            sc = term if sc is None else sc + term
        if diag:
            sc = jnp.where(causal, sc, -jnp.inf)
        sc_ref[c] = sc
        half_ref[c] = sc.astype(BF16)

    def score_body(c, carry):
        score_chunk(c, False)
        return carry

    lax.fori_loop(0, i, score_body, 0)
    score_chunk(i, True)
    nch = i + 1

    def count(pred):
        def body(c, acc):
            v = jnp.where(pred(sc_ref[c]), 1.0, 0.0)
            return acc + jnp.sum(v.reshape(t // SUBLANES, SUBLANES, t), axis=0)
        acc = lax.fori_loop(0, nch, body, jnp.zeros((SUBLANES, t), F32))
        return jnp.sum(acc, axis=0, keepdims=True)

    def ordered_bits_to_float(u):
        return lax.bitcast_convert_type(jnp.where(u < 0, u ^ 0x7FFFFFFF, u), F32)

    def count_half(cand):
        cand_rows = jnp.broadcast_to(cand, (PACKED_ROWS, t))
        one = jnp.ones((PACKED_ROWS, t), BF16)
        zero = jnp.zeros((PACKED_ROWS, t), BF16)

        def body(c, accs):
            accs = list(accs)
            for k in range(t // PACKED_ROWS):
                x = half_ref[c, k * PACKED_ROWS:(k + 1) * PACKED_ROWS, :]
                accs[k % len(accs)] = accs[k % len(accs)] + jnp.where(x >= cand_rows, one, zero)
            return tuple(accs)

        accs = lax.fori_loop(0, nch, body, (zero,) * 4)
        total = sum(a.astype(F32) for a in accs)
        return jnp.sum(total, axis=0, keepdims=True)

    def half_step(jj, prefix):
        cand = prefix + jnp.left_shift(jnp.int32(1), 15 - jj)
        bits = jnp.bitwise_and(jnp.where(cand < 0, cand ^ 0x7FFF, cand), 0xFFFF)
        cand_f = lax.bitcast_convert_type(jnp.left_shift(bits, 16), F32).astype(BF16)
        return jnp.where(count_half(cand_f) >= topk, cand, prefix)

    half = lax.fori_loop(0, 16, half_step, jnp.full((1, t), -HALF_SPAN, I32))

    lo = jnp.maximum(jnp.left_shift(jnp.maximum(half - 1, -HALF_SPAN), 16), NEG_INF_BITS)
    hi = jnp.where(half + 2 >= HALF_SPAN, INT_MAX, jnp.left_shift(half + 2, 16))

    def range_step(_, carry):
        lo, hi = carry
        mid = lo + jnp.right_shift(hi - lo, 1)
        enough = count(lambda x: x >= ordered_bits_to_float(mid)) >= topk
        return jnp.where(enough, mid, lo), jnp.where(enough, hi, mid)

    tau_bits, _ = lax.fori_loop(0, RANGE_STEPS, range_step, (lo, hi))
    tau = ordered_bits_to_float(tau_bits)
    q_pos = i * t + lax.broadcasted_iota(I32, (1, t), 1)
    tau = jnp.where(q_pos + 1 < int(topk), -F32_MAX, tau)

    above = count(lambda x: x > tau)
    delta_ref[0:1, :] = jnp.zeros((1, t), F32)
    delta_ref[1:2, :] = topk - above

    @pl.when(jnp.max(jnp.where(above >= topk, 1, 0)) > 0)
    def _():
        gap = ordered_bits_to_float(tau_bits + 1) - tau

        def delta_step(jj, carry):
            delta, step = carry
            cand = delta + step
            cnt = count(lambda x: x - tau >= cand)
            return jnp.where(cnt >= topk, cand, delta), step * 0.5

        delta, _ = lax.fori_loop(0, GAP_STEPS, delta_step, (jnp.zeros((1, t), F32), gap * 0.5))
        delta_ref[0:1, :] = delta
        delta_ref[1:2, :] = topk - count(lambda x: x - tau > delta)

    delta = delta_ref[0:1, :]
    need = delta_ref[1:2, :]

    lower = jnp.where(qs < ks, 1.0, 0.0).astype(BF16)

    def mask_chunk(c, before, diag):
        d = sc_ref[c] - tau
        eq = d == delta
        eqf = jnp.where(eq, 1.0, 0.0)
        rank = jnp.dot(lower, eqf.astype(BF16), preferred_element_type=F32) + before
        tie = jnp.where(rank < need, 0.0, NEG)
        madd = jnp.where(d > delta, 0.0, jnp.where(eq, tie, NEG))
        if diag:
            madd = jnp.where(causal, madd, NEG)
        sc_ref[c] = madd
        return before + jnp.sum(eqf, axis=0, keepdims=True)

    before = lax.fori_loop(0, i, lambda c, b: mask_chunk(c, b, False), jnp.zeros((1, t), F32))
    mask_chunk(i, before, True)

    npairs = N_DSA_HEADS // 2
    _init_softmax(m_ref, l_ref, acc_ref)

    def scores(c, pair, s_ref, mx_ref):
        off = pl.multiple_of(c * t, t)
        kt = kd_ref[0, pl.ds(off, t), pair * LANES:(pair + 1) * LANES]
        madd = sc_ref[c]
        for e in range(2):
            s_ref[e] = lax.dot_general(kt, qm_ref[2 * pair + e], _NT,
                                       preferred_element_type=F32) + madd

    def softmax(pair, s_ref, mx_ref, p_ref, a_ref, near):
        bias = None
        if near is not None:
            bias = lambda e, r, block: block + _bias_rows(
                tab_ref, 2 * pair + e, t, near == 1, r, block.shape[0])
        _softmax_stage(pair, s_ref, None, p_ref, a_ref, m_ref, l_ref, bias)

    def values(c, pair, p_ref, a_ref):
        vt = vt_ref[c, pair * LANES:(pair + 1) * LANES, :]
        _values_stage(pair, vt, p_ref, a_ref, acc_ref)

    _causal_attention(i, npairs, scores, softmax, values, ((s_a, s_b), (mx_a, mx_b), (p_a, p_b), (a_a, a_b)))
    for pair in range(npairs):
        o_ref[0, :, pair * LANES:(pair + 1) * LANES] = _pair_output(
            pair, l_ref, acc_ref).astype(o_ref.dtype)


def _dsa(qd, qi, small, kd, vdt, kk, tab):
    B, S, W = qd.shape
    t = ATT_T
    nk = S // t
    return pl.pallas_call(
        _dsa_kernel,
        out_shape=jax.ShapeDtypeStruct((B, S, W), BF16),
        grid=(B, nk),
        in_specs=[pl.BlockSpec((1, t, W), lambda b, i: (b, i, 0)),
                  pl.BlockSpec((1, t, qi.shape[2]), lambda b, i: (b, i, 0)),
                  pl.BlockSpec((1, SUBLANES, t), lambda b, i: (b, 1, i)),
                  _resident((1, S, W), lambda b, i: (b, 0, 0)),
                  _resident((nk, W, t), lambda b, i: (b, 0, 0)),
                  _resident((1, S, LANES), lambda b, i: (b, 0, 0)),
                  _resident(tab.shape, lambda b, i: (0, 0, 0, 0))],
        out_specs=pl.BlockSpec((1, t, W), lambda b, i: (b, i, 0)),
        scratch_shapes=[pltpu.VMEM((nk, t, t), F32),
                        pltpu.VMEM((nk, t, t), BF16),
                        pltpu.VMEM((2, t), F32),
                        pltpu.VMEM((N_DSA_HEADS, t, LANES), BF16),
                        pltpu.VMEM((N_IDX_HEADS, t, LANES), BF16)]
                       + _attention_scratch(t, N_DSA_HEADS),
        compiler_params=pltpu.CompilerParams(
            dimension_semantics=("arbitrary", "arbitrary"),
            vmem_limit_bytes=VMEM_LIMIT),
        name="dsa",
    )(qd, qi, small, kd, vdt, kk, tab)


def _post_kernel(x_ref, yf_ref, yd_ref, sgf_ref, sgd_ref, ada_ref, g2_ref, gf_ref,
                 wbf_ref, wbd_ref, wo_ref, w1_ref, w2_ref, o_ref):
    bf = jnp.dot(yf_ref[...], wbf_ref[...], preferred_element_type=F32)
    bd = jnp.dot(yd_ref[...], wbd_ref[...], preferred_element_type=F32)
    merged = sgf_ref[...].astype(F32) * bf + sgd_ref[...].astype(F32) * bd
    o = jnp.dot(merged.astype(BF16), wo_ref[...], preferred_element_type=F32)
    x1 = x_ref[...] + ada_ref[0, 2:3, :] * o
    h2 = _rms_mod(x1, g2_ref[...], ada_ref[0, 4:5, :], ada_ref[0, 3:4, :]).astype(BF16)
    acc = jnp.zeros(x1.shape, F32)
    for k in range(w1_ref.shape[1] // FF_CHUNK):
        u = jnp.dot(h2, w1_ref[:, k * FF_CHUNK:(k + 1) * FF_CHUNK], preferred_element_type=F32)
        u = jnp.square(jnp.maximum(u, 0.0)).astype(BF16)
        acc = acc + jnp.dot(u, w2_ref[k * FF_CHUNK:(k + 1) * FF_CHUNK, :],
                            preferred_element_type=F32)
    x2 = x1 + ada_ref[0, 5:6, :] * acc
    y = x2 * lax.rsqrt(jnp.mean(x2 * x2, axis=-1, keepdims=True) + EPS) * gf_ref[...]
    o_ref[...] = y


def _post(x2d, yf, yd, sgf, sgd, ada, g2, gfin, wbf, wbd, wo, w1, w2, S):
    M, D = x2d.shape
    tm = POST_TM
    tpb = S // tm
    rows = lambda w: pl.BlockSpec((tm, w), lambda i: (i, 0))
    full = lambda a: _resident(a.shape, lambda i: (0,) * a.ndim)
    return pl.pallas_call(
        _post_kernel,
        out_shape=jax.ShapeDtypeStruct((M, D), F32),
        grid=(M // tm,),
        in_specs=[rows(D), rows(yf.shape[1]), rows(yd.shape[1]), rows(D), rows(D),
                  pl.BlockSpec((1,) + ada.shape[1:], lambda i: (i // tpb, 0, 0)),
                  pl.BlockSpec((1, D), lambda i: (0, 0)),
                  pl.BlockSpec((1, D), lambda i: (0, 0)),
                  full(wbf), full(wbd), full(wo), full(w1), full(w2)],
        out_specs=rows(D),
        compiler_params=pltpu.CompilerParams(dimension_semantics=("arbitrary",),
                                             vmem_limit_bytes=VMEM_LIMIT),
        name="post",
    )(x2d, yf, yd, sgf, sgd, ada, g2, gfin, wbf, wbd, wo, w1, w2)


def kernel(x, c, w_ada, b_ada, g_norm1, w_in, b_forget, rel_bias, w_branch_fox,
           w_branch_dsa, w_out, g_norm2, w_mlp1, w_mlp2, g_final):
    B, S, D = x.shape
    fw = N_FOX_HEADS * HEAD_DIM
    dw = N_DSA_HEADS * HEAD_DIM
    iw = N_IDX_HEADS * IDX_DIM
    sizes = [fw, fw, fw, N_FOX_HEADS, dw, dw, dw, iw, IDX_DIM, N_IDX_HEADS, D, D]
    offs = [0]
    for s_ in sizes:
        offs.append(offs[-1] + s_)
    seg = lambda k: w_in[:, offs[k]:offs[k + 1]]
    q_scale = HEAD_DIM ** -0.5 * LOG2E
    i_scale = IDX_DIM ** -0.5
    w_main = jnp.concatenate(
        [seg(0) * q_scale, seg(1), seg(4) * q_scale, seg(5),
         seg(7) * i_scale, seg(8), seg(8), seg(10), seg(11)], axis=1).astype(BF16)
    w_t = jnp.concatenate([seg(2), seg(6)], axis=1).T.astype(BF16)
    w_small_t = (jnp.zeros((SMALL_ROWS, D), F32)
                 .at[:N_FOX_HEADS].set(seg(3).T)
                 .at[SUBLANES:SUBLANES + N_IDX_HEADS].set(seg(9).T)).astype(BF16)

    ada = _ada(c, w_ada, b_ada)
    tab = _biastab(rel_bias, BIAS_T)
    x2d = x.reshape(B * S, D)
    (qf, kf, qd, kd, qi, kk, sgf, sgd, vft, vdt, small) = _proj(
        x2d, ada, g_norm1.reshape(1, D), w_main, w_t, w_small_t, B, S)
    fk = _fcum(small, b_forget)
    r3 = lambda a: a.reshape(B, S, a.shape[-1])
    y_fox = _fox(r3(qf), r3(kf), vft, fk)
    y_dsa = _dsa(r3(qd), r3(qi), small, r3(kd), vdt, r3(kk), tab)
    out = _post(x2d, y_fox.reshape(B * S, fw), y_dsa.reshape(B * S, dw), sgf, sgd, ada,
                g_norm2.reshape(1, D), g_final.reshape(1, D),
                w_branch_fox.astype(BF16), w_branch_dsa.astype(BF16), w_out.astype(BF16),
                w_mlp1.astype(BF16), w_mlp2.astype(BF16), S)
    return out.reshape(B, S, D)
```

```python
import math

import jax
import jax.numpy as jnp
import numpy as np
from jax import lax
from jax.experimental import pallas as pl
from jax.experimental.pallas import tpu as pltpu

F32 = jnp.float32
BF16 = jnp.bfloat16
I32 = jnp.int32

HEAD_DIM = 64
N_FOX_HEADS = 8
N_DSA_HEADS = 8
N_IDX_HEADS = 4
IDX_DIM = 64
TOPK_MAX = 256
N_BUCKETS = 32
MAX_DISTANCE = 128
EPS = 1e-6
LOG2E = float(np.float32(math.log2(math.e)))

LANES = 128
SUBLANES = 8
PACKED_ROWS = 16
INT_MIN = -2 ** 31
INT_MAX = 2 ** 31 - 1
HALF_SPAN = 2 ** 15
NEG_INF_BITS = INT_MIN + 0x7FFFFF
RANGE_STEPS = 18
F32_MAX = float(np.finfo(np.float32).max)
NEG = -1e30
VMEM_LIMIT = 52 * 1024 * 1024

ADA_COLS = 2048
PROJ_TM = 512
ATT_T = 512
BIAS_T = MAX_DISTANCE
POST_TM = 512
FF_CHUNK = 1024
CUM_CHUNK = 256
SMALL_ROWS = 16
SOFTMAX_ROWS = 128
F_TERMS = 3
GAP_STEPS = 30

_NT = (((1,), (1,)), ((), ()))


def _resident(block_shape, index_map):
    return pl.BlockSpec(block_shape, index_map, pipeline_mode=pl.Buffered(1))


def _ada_kernel(c_ref, w_ref, b_ref, o_ref):
    o_ref[...] = jnp.dot(c_ref[...], w_ref[...], preferred_element_type=F32,
                         precision=lax.Precision.HIGHEST) + b_ref[...]


def _ada(c, w_ada, b_ada):
    B, D = c.shape
    n = w_ada.shape[1]
    rows = SUBLANES
    c_pad = jnp.zeros((rows, D), F32).at[:B].set(c)
    out = pl.pallas_call(
        _ada_kernel,
        out_shape=jax.ShapeDtypeStruct((rows, n), F32),
        grid=(n // ADA_COLS,),
        in_specs=[pl.BlockSpec((rows, D), lambda j: (0, 0)),
                  pl.BlockSpec((D, ADA_COLS), lambda j: (0, j)),
                  pl.BlockSpec((1, ADA_COLS), lambda j: (0, j))],
        out_specs=pl.BlockSpec((rows, ADA_COLS), lambda j: (0, j)),
        compiler_params=pltpu.CompilerParams(vmem_limit_bytes=VMEM_LIMIT),
        name="ada",
    )(c_pad, w_ada, b_ada.reshape(1, n))
    return out[:B].reshape(B, n // D, D)


def _bucket_tiles(t):
    ks = np.arange(t, dtype=np.int32)[None, :, None]
    qs = np.arange(t, dtype=np.int32)[None, None, :]
    n = np.maximum(qs - ks + np.arange(2, dtype=np.int32)[:, None, None] * t, 0)
    max_exact = N_BUCKETS // 2
    nf = np.maximum(n, 1).astype(np.float32)
    scaled = (np.log(nf / np.float32(max_exact))
              / np.float32(math.log(MAX_DISTANCE / max_exact))
              * np.float32(N_BUCKETS - max_exact)).astype(np.float32)
    large = np.minimum(max_exact + scaled.astype(np.int32), N_BUCKETS - 1)
    return np.where(n < max_exact, n, large).astype(np.int32)


def _biastab_kernel(rb_ref, bucket_ref, o_ref):
    bucket = bucket_ref[0]
    for h in range(o_ref.shape[0]):
        acc = jnp.zeros(bucket.shape, F32)
        for b in range(N_BUCKETS):
            acc = jnp.where(bucket == b, rb_ref[b, h], acc)
        o_ref[h, 0] = (acc - rb_ref[N_BUCKETS - 1, h]) * LOG2E


def _biastab(rel_bias, t):
    nh = rel_bias.shape[1]
    return pl.pallas_call(
        _biastab_kernel,
        out_shape=jax.ShapeDtypeStruct((nh, 2, t, t), F32),
        grid=(2,),
        in_specs=[pl.BlockSpec(memory_space=pltpu.SMEM),
                  pl.BlockSpec((1, t, t), lambda e: (e, 0, 0))],
        out_specs=pl.BlockSpec((nh, 1, t, t), lambda e: (0, e, 0, 0)),
        name="biastab",
    )(rel_bias, jnp.asarray(_bucket_tiles(t)))


def _rms_mod(x, g, scale, shift):
    y = x * lax.rsqrt(jnp.mean(x * x, axis=-1, keepdims=True) + EPS) * g
    return y * (1.0 + scale) + shift


def _proj_kernel(x_ref, ada_ref, g_ref, w_ref, wt_ref, ws_ref,
                 qf_ref, kf_ref, qd_ref, kd_ref, qi_ref, kk_ref, sgf_ref, sgd_ref,
                 vft_ref, vdt_ref, small_ref):
    h = _rms_mod(x_ref[...], g_ref[...], ada_ref[0, 1:2, :], ada_ref[0, 0:1, :])
    hb = h.astype(BF16)
    col = 0
    for ref in (qf_ref, kf_ref, qd_ref, kd_ref, qi_ref, kk_ref):
        n = ref.shape[-1]
        r = jnp.dot(hb, w_ref[:, col:col + n], preferred_element_type=F32)
        ref[...] = r.astype(ref.dtype)
        col += n
    for ref in (sgf_ref, sgd_ref):
        n = ref.shape[-1]
        r = jnp.dot(hb, w_ref[:, col:col + n], preferred_element_type=F32)
        ref[...] = (1.0 / (1.0 + jnp.exp(-r))).astype(ref.dtype)
        col += n
    row = 0
    for ref in (vft_ref, vdt_ref):
        nt, n, t = ref.shape
        r = lax.dot_general(wt_ref[row:row + n, :], hb, _NT, preferred_element_type=F32)
        for u in range(nt):
            ref[u] = r[:, u * t:(u + 1) * t].astype(ref.dtype)
        row += n
    small_ref[0] = lax.dot_general(ws_ref[...], hb, _NT, preferred_element_type=F32)


def _proj(x2d, ada, g1, w_main, w_t, w_small_t, B, S):
    M, D = x2d.shape
    tm = PROJ_TM
    tpb = S // tm
    t = ATT_T
    fw = N_FOX_HEADS * HEAD_DIM
    iw = N_IDX_HEADS * IDX_DIM
    widths = [fw, fw, fw, fw, iw, LANES, D, D]
    out_shape = [jax.ShapeDtypeStruct((M, w), BF16) for w in widths]
    out_specs = [pl.BlockSpec((tm, w), lambda i: (i, 0)) for w in widths]
    for _ in range(2):
        out_shape.append(jax.ShapeDtypeStruct((M // t, fw, t), BF16))
        out_specs.append(pl.BlockSpec((tm // t, fw, t), lambda i: (i, 0, 0)))
    out_shape.append(jax.ShapeDtypeStruct((B, SMALL_ROWS, S), F32))
    out_specs.append(pl.BlockSpec((1, SMALL_ROWS, tm), lambda i: (i // tpb, 0, i % tpb)))
    return pl.pallas_call(
        _proj_kernel,
        out_shape=out_shape,
        grid=(M // tm,),
        in_specs=[pl.BlockSpec((tm, D), lambda i: (i, 0)),
                  pl.BlockSpec((1,) + ada.shape[1:], lambda i: (i // tpb, 0, 0)),
                  pl.BlockSpec((1, D), lambda i: (0, 0)),
                  _resident(w_main.shape, lambda i: (0, 0)),
                  _resident(w_t.shape, lambda i: (0, 0)),
                  _resident(w_small_t.shape, lambda i: (0, 0))],
        out_specs=out_specs,
        compiler_params=pltpu.CompilerParams(dimension_semantics=("arbitrary",),
                                             vmem_limit_bytes=VMEM_LIMIT),
        name="proj",
    )(x2d, ada, g1, w_main, w_t, w_small_t)


def _fcum_kernel(f_ref, b_ref, o_ref):
    x = f_ref[0] + b_ref[...]
    nh = x.shape[0]
    logf = -(jnp.maximum(-x, 0.0) + jnp.log1p(jnp.exp(-jnp.abs(x))))
    ch = CUM_CHUNK
    r = lax.broadcasted_iota(I32, (ch, ch), 0)
    c = lax.broadcasted_iota(I32, (ch, ch), 1)
    upper = (r <= c).astype(F32)
    carry = jnp.zeros((nh, 1), F32)
    for k in range(x.shape[1] // ch):
        seg = logf[:, k * ch:(k + 1) * ch]
        cs = jnp.dot(seg, upper, preferred_element_type=F32,
                     precision=lax.Precision.HIGHEST) + carry
        carry = cs[:, ch - 1:ch]
        rest = -cs * LOG2E
        terms = []
        for _ in range(F_TERMS):
            term = rest.astype(BF16).astype(F32)
            terms.append(term)
            rest = rest - term
        rows = jnp.concatenate(terms + [jnp.zeros((LANES - F_TERMS * nh, ch), F32)], axis=0)
        o_ref[0, k * ch:(k + 1) * ch, :] = rows.T.astype(BF16)


def _fcum(small, b_forget):
    B, _, S = small.shape
    H = N_FOX_HEADS
    return pl.pallas_call(
        _fcum_kernel,
        out_shape=jax.ShapeDtypeStruct((B, S, LANES), BF16),
        grid=(B,),
        in_specs=[pl.BlockSpec((1, H, S), lambda b: (b, 0, 0)),
                  pl.BlockSpec((H, 1), lambda b: (0, 0))],
        out_specs=pl.BlockSpec((1, S, LANES), lambda b: (b, 0, 0)),
        name="fcum",
    )(small, b_forget.reshape(H, 1))


def _tree(op, xs):
    xs = list(xs)
    while len(xs) > 1:
        xs = [op(xs[k], xs[k + 1]) if k + 1 < len(xs) else xs[k] for k in range(0, len(xs), 2)]
    return xs[0]


def _fold_rows(x, reduce):
    return reduce(x.reshape(x.shape[0] // SUBLANES, SUBLANES, x.shape[1]), axis=0)


def _column_max(block_of):
    t = ATT_T
    parts = [_fold_rows(block_of(r), jnp.max) for r in range(0, t, SOFTMAX_ROWS)]
    return jnp.max(_tree(jnp.maximum, parts), axis=0, keepdims=True)


def _scores_out(st, e, s_ref, mx_ref):
    s_ref[e] = st
    mx_ref[e] = _column_max(lambda r: st[r:r + SOFTMAX_ROWS])


def _softmax_stage(pair, s_ref, mx_ref, p_ref, a_ref, m_ref, l_ref, extra):
    t = s_ref.shape[1]
    rb = SOFTMAX_ROWS
    for e in range(2):
        h = 2 * pair + e
        if extra is not None:
            for r in range(0, t, rb):
                s_ref[e, r:r + rb, :] = extra(e, r, s_ref[e, r:r + rb, :])
        if extra is None and mx_ref is not None:
            m_cur = mx_ref[e]
        else:
            m_cur = _column_max(lambda r: s_ref[e, r:r + rb, :])
        m_prev = m_ref[h]
        m_new = jnp.maximum(m_prev, m_cur)
        alpha = jnp.exp2(m_prev - m_new)
        half = t // 2
        sums = []
        for q0 in range(0, t, half):
            parts = []
            for r in range(0, t, rb):
                p = jnp.exp2(s_ref[e, r:r + rb, q0:q0 + half] - m_new[:, q0:q0 + half])
                parts.append(_fold_rows(p, jnp.sum))
                p_ref[e, r:r + rb, q0:q0 + half] = p.astype(BF16)
            sums.append(jnp.sum(_tree(jnp.add, parts), axis=0, keepdims=True))
        l_ref[h] = alpha * l_ref[h] + jnp.concatenate(sums, axis=1)
        m_ref[h] = m_new
        a_ref[e] = alpha


def _values_stage(pair, vt, p_ref, a_ref, acc_ref):
    for e in range(2):
        h = 2 * pair + e
        ve = vt[e * HEAD_DIM:(e + 1) * HEAD_DIM]
        acc_ref[h] = a_ref[e] * acc_ref[h] + jnp.dot(ve, p_ref[e], preferred_element_type=F32)


def _causal_attention(i, npairs, scores, softmax, values, bufs):
    s_bufs, mx_bufs, p_bufs, a_bufs = bufs
    p_bufs[1][...] = jnp.zeros(p_bufs[1].shape, BF16)
    a_bufs[1][...] = jnp.ones(a_bufs[1].shape, F32)
    scores(0, 0, s_bufs[0], mx_bufs[0])

    def tile_steps(c, near, last_tile):
        for pair in range(npairs):
            cur, oth = pair % 2, 1 - pair % 2
            softmax(pair, s_bufs[cur], mx_bufs[cur], p_bufs[cur], a_bufs[cur], near)
            if pair + 1 < npairs:
                scores(c, pair + 1, s_bufs[oth], mx_bufs[oth])
            elif not last_tile:
                scores(c + 1, 0, s_bufs[oth], mx_bufs[oth])
            if pair >= 1:
                values(c, pair - 1, p_bufs[oth], a_bufs[oth])
            else:
                values(jnp.maximum(c - 1, 0), npairs - 1, p_bufs[oth], a_bufs[oth])

    def far_tile(c, carry):
        tile_steps(c, None, False)
        return carry

    lax.fori_loop(0, jnp.maximum(i - 1, 0), far_tile, 0)

    @pl.when(i >= 1)
    def _():
        tile_steps(i - 1, 1, False)

    tile_steps(i, 0, True)
    last = (npairs - 1) % 2
    values(i, npairs - 1, p_bufs[last], a_bufs[last])


def _attention_scratch(t, nheads):
    pair = lambda dt: pltpu.VMEM((2, t, t), dt)
    row = pltpu.VMEM((2, 1, t), F32)
    return [pair(F32), pair(F32), row, row,
            pair(BF16), pair(BF16), row, row,
            pltpu.VMEM((nheads, HEAD_DIM, t), F32),
            pltpu.VMEM((nheads, 1, t), F32), pltpu.VMEM((nheads, 1, t), F32)]


def _init_softmax(m_ref, l_ref, acc_ref):
    m_ref[...] = jnp.full(m_ref.shape, NEG, F32)
    l_ref[...] = jnp.zeros(l_ref.shape, F32)
    acc_ref[...] = jnp.zeros(acc_ref.shape, F32)


def _pair_output(pair, l_ref, acc_ref):
    h = 2 * pair
    out_t = jnp.concatenate([acc_ref[h] / l_ref[h], acc_ref[h + 1] / l_ref[h + 1]], axis=0)
    return out_t.T


def _split_pair(qp, lane):
    qf = qp.astype(F32)
    return (jnp.where(lane < HEAD_DIM, qf, 0.0).astype(qp.dtype),
            jnp.where(lane >= HEAD_DIM, qf, 0.0).astype(qp.dtype))


def _fox_kernel(q_ref, k_ref, vt_ref, f_ref, o_ref, qm_ref,
                s_a, s_b, mx_a, mx_b, p_a, p_b, a_a, a_b, acc_ref, m_ref, l_ref):
    i = pl.program_id(1)
    t = q_ref.shape[1]
    npairs = N_FOX_HEADS // 2
    lane = lax.broadcasted_iota(I32, (t, LANES), 1)
    for pair in range(npairs):
        halves = _split_pair(q_ref[0, :, pair * LANES:(pair + 1) * LANES], lane)
        for e in range(2):
            h = 2 * pair + e
            ones = jnp.where(lane % N_FOX_HEADS == h, 1.0, 0.0)
            ones = jnp.where(lane < F_TERMS * N_FOX_HEADS, ones, 0.0).astype(BF16)
            qm_ref[h, :, 0:LANES] = halves[e]
            qm_ref[h, :, LANES:2 * LANES] = ones
    _init_softmax(m_ref, l_ref, acc_ref)

    def scores(c, pair, s_ref, mx_ref):
        off = pl.multiple_of(c * t, t)
        kt = jnp.concatenate([k_ref[0, pl.ds(off, t), pair * LANES:(pair + 1) * LANES],
                              f_ref[0, pl.ds(off, t), :]], axis=1)
        for e in range(2):
            st = lax.dot_general(kt, qm_ref[2 * pair + e], _NT, preferred_element_type=F32)
            _scores_out(st, e, s_ref, mx_ref)

    def causal(e, r, block):
        ks = lax.broadcasted_iota(I32, block.shape, 0) + r
        qs = lax.broadcasted_iota(I32, block.shape, 1)
        return jnp.where(ks <= qs, block, NEG)

    def softmax(pair, s_ref, mx_ref, p_ref, a_ref, near):
        _softmax_stage(pair, s_ref, mx_ref, p_ref, a_ref, m_ref, l_ref, causal if near == 0 else None)

    def values(c, pair, p_ref, a_ref):
        vt = vt_ref[c, pair * LANES:(pair + 1) * LANES, :]
        _values_stage(pair, vt, p_ref, a_ref, acc_ref)

    _causal_attention(i, npairs, scores, softmax, values, ((s_a, s_b), (mx_a, mx_b), (p_a, p_b), (a_a, a_b)))
    for pair in range(npairs):
        o_ref[0, :, pair * LANES:(pair + 1) * LANES] = _pair_output(
            pair, l_ref, acc_ref).astype(o_ref.dtype)


def _fox(qf, kf, vft, fk):
    B, S, W = qf.shape
    t = ATT_T
    nk = S // t
    return pl.pallas_call(
        _fox_kernel,
        out_shape=jax.ShapeDtypeStruct((B, S, W), BF16),
        grid=(B, nk),
        in_specs=[pl.BlockSpec((1, t, W), lambda b, i: (b, i, 0)),
                  _resident((1, S, W), lambda b, i: (b, 0, 0)),
                  _resident((nk, W, t), lambda b, i: (b, 0, 0)),
                  _resident((1, S, LANES), lambda b, i: (b, 0, 0))],
        out_specs=pl.BlockSpec((1, t, W), lambda b, i: (b, i, 0)),
        scratch_shapes=[pltpu.VMEM((N_FOX_HEADS, t, 2 * LANES), BF16)]
                       + _attention_scratch(t, N_FOX_HEADS),
        compiler_params=pltpu.CompilerParams(
            dimension_semantics=("arbitrary", "arbitrary"),
            vmem_limit_bytes=VMEM_LIMIT),
        name="fox",
    )(qf, kf, vft, fk)


def _bias_rows(tab_ref, h, t, previous, r, rows):
    nb = t // BIAS_T
    a, r0 = divmod(r, BIAS_T)
    zero = jnp.zeros((rows, BIAS_T), F32)
    blocks = []
    for b in range(nb):
        d = b - a + (nb if previous else 0)
        blocks.append(tab_ref[h, d, r0:r0 + rows, :] if d in (0, 1) else zero)
    return jnp.concatenate(blocks, axis=1)


def _dsa_kernel(qd_ref, qi_ref, w_ref, kd_ref, vt_ref, kk_ref, tab_ref, o_ref,
                sc_ref, half_ref, delta_ref, qm_ref, qim_ref,
                s_a, s_b, mx_a, mx_b, p_a, p_b, a_a, a_b, acc_ref, m_ref, l_ref):
    i = pl.program_id(1)
    t = qd_ref.shape[1]
    S = kd_ref.shape[1]
    topk = float(min(TOPK_MAX, S // 4))
    lane = lax.broadcasted_iota(I32, (t, LANES), 1)
    ks = lax.broadcasted_iota(I32, (t, t), 0)
    qs = lax.broadcasted_iota(I32, (t, t), 1)
    causal = ks <= qs

    for p in range(N_DSA_HEADS // 2):
        a, b = _split_pair(qd_ref[0, :, p * LANES:(p + 1) * LANES], lane)
        qm_ref[2 * p] = a
        qm_ref[2 * p + 1] = b
    for p in range(N_IDX_HEADS // 2):
        a, b = _split_pair(qi_ref[0, :, p * LANES:(p + 1) * LANES], lane)
        qim_ref[2 * p] = a
        qim_ref[2 * p + 1] = b
    wrows = [w_ref[0, h:h + 1, :] * (N_IDX_HEADS ** -0.5) for h in range(N_IDX_HEADS)]

    def score_chunk(c, diag):
        kt = kk_ref[0, pl.ds(pl.multiple_of(c * t, t), t), :]
        sc = None
        for h in range(N_IDX_HEADS):
            d = lax.dot_general(kt, qim_ref[h], _NT, preferred_element_type=F32)
            term = jnp.maximum(d, 0.0) * wrows[h]
            sc = term if sc is None else sc + term
        if diag:
            sc = jnp.where(causal, sc, -jnp.inf)
        sc_ref[c] = sc
        half_ref[c] = sc.astype(BF16)

    def score_body(c, carry):
        score_chunk(c, False)
        return carry

    lax.fori_loop(0, i, score_body, 0)
    score_chunk(i, True)
    nch = i + 1

    def count(pred):
        def body(c, acc):
            v = jnp.where(pred(sc_ref[c]), 1.0, 0.0)
            return acc + jnp.sum(v.reshape(t // SUBLANES, SUBLANES, t), axis=0)
        acc = lax.fori_loop(0, nch, body, jnp.zeros((SUBLANES, t), F32))
        return jnp.sum(acc, axis=0, keepdims=True)

    def ordered_bits_to_float(u):
        return lax.bitcast_convert_type(jnp.where(u < 0, u ^ 0x7FFFFFFF, u), F32)

    def count_half(cand):
        cand_rows = jnp.broadcast_to(cand, (PACKED_ROWS, t))
        one = jnp.ones((PACKED_ROWS, t), BF16)
        zero = jnp.zeros((PACKED_ROWS, t), BF16)

        def body(c, accs):
            accs = list(accs)
            for k in range(t // PACKED_ROWS):
                x = half_ref[c, k * PACKED_ROWS:(k + 1) * PACKED_ROWS, :]
                accs[k % len(accs)] = accs[k % len(accs)] + jnp.where(x >= cand_rows, one, zero)
            return tuple(accs)

        accs = lax.fori_loop(0, nch, body, (zero,) * 4)
        total = sum(a.astype(F32) for a in accs)
        return jnp.sum(total, axis=0, keepdims=True)

    def half_step(jj, prefix):
        cand = prefix + jnp.left_shift(jnp.int32(1), 15 - jj)
        bits = jnp.bitwise_and(jnp.where(cand < 0, cand ^ 0x7FFF, cand), 0xFFFF)
        cand_f = lax.bitcast_convert_type(jnp.left_shift(bits, 16), F32).astype(BF16)
        return jnp.where(count_half(cand_f) >= topk, cand, prefix)

    half = lax.fori_loop(0, 16, half_step, jnp.full((1, t), -HALF_SPAN, I32))

    lo = jnp.maximum(jnp.left_shift(jnp.maximum(half - 1, -HALF_SPAN), 16), NEG_INF_BITS)
    hi = jnp.where(half + 2 >= HALF_SPAN, INT_MAX, jnp.left_shift(half + 2, 16))

    def range_step(_, carry):
        lo, hi = carry
        mid = lo + jnp.right_shift(hi - lo, 1)
        enough = count(lambda x: x >= ordered_bits_to_float(mid)) >= topk
        return jnp.where(enough, mid, lo), jnp.where(enough, hi, mid)

    tau_bits, _ = lax.fori_loop(0, RANGE_STEPS, range_step, (lo, hi))
    tau = ordered_bits_to_float(tau_bits)
    q_pos = i * t + lax.broadcasted_iota(I32, (1, t), 1)
    tau = jnp.where(q_pos + 1 < int(topk), -F32_MAX, tau)

    above = count(lambda x: x > tau)
    delta_ref[0:1, :] = jnp.zeros((1, t), F32)
    delta_ref[1:2, :] = topk - above

    @pl.when(jnp.max(jnp.where(above >= topk, 1, 0)) > 0)
    def _():
        gap = ordered_bits_to_float(tau_bits + 1) - tau

        def delta_step(jj, carry):
            delta, step = carry
            cand = delta + step
            cnt = count(lambda x: x - tau >= cand)
            return jnp.where(cnt >= topk, cand, delta), step * 0.5

        delta, _ = lax.fori_loop(0, GAP_STEPS, delta_step, (jnp.zeros((1, t), F32), gap * 0.5))
        delta_ref[0:1, :] = delta
        delta_ref[1:2, :] = topk - count(lambda x: x - tau > delta)

    delta = delta_ref[0:1, :]
    need = delta_ref[1:2, :]

    lower = jnp.where(qs < ks, 1.0, 0.0).astype(BF16)

    def mask_chunk(c, before, diag):
        d = sc_ref[c] - tau
        eq = d == delta
        eqf = jnp.where(eq, 1.0, 0.0)
        rank = jnp.dot(lower, eqf.astype(BF16), preferred_element_type=F32) + before
        tie = jnp.where(rank < need, 0.0, NEG)
        madd = jnp.where(d > delta, 0.0, jnp.where(eq, tie, NEG))
        if diag:
            madd = jnp.where(causal, madd, NEG)
        sc_ref[c] = madd
        return before + jnp.sum(eqf, axis=0, keepdims=True)

    before = lax.fori_loop(0, i, lambda c, b: mask_chunk(c, b, False), jnp.zeros((1, t), F32))
    mask_chunk(i, before, True)

    npairs = N_DSA_HEADS // 2
    _init_softmax(m_ref, l_ref, acc_ref)

    def scores(c, pair, s_ref, mx_ref):
        off = pl.multiple_of(c * t, t)
        kt = kd_ref[0, pl.ds(off, t), pair * LANES:(pair + 1) * LANES]
        madd = sc_ref[c]
        for e in range(2):
            s_ref[e] = lax.dot_general(kt, qm_ref[2 * pair + e], _NT,
                                       preferred_element_type=F32) + madd

    def softmax(pair, s_ref, mx_ref, p_ref, a_ref, near):
        bias = None
        if near is not None:
            bias = lambda e, r, block: block + _bias_rows(
                tab_ref, 2 * pair + e, t, near == 1, r, block.shape[0])
        _softmax_stage(pair, s_ref, None, p_ref, a_ref, m_ref, l_ref, bias)

    def values(c, pair, p_ref, a_ref):
        vt = vt_ref[c, pair * LANES:(pair + 1) * LANES, :]
        _values_stage(pair, vt, p_ref, a_ref, acc_ref)

    _causal_attention(i, npairs, scores, softmax, values, ((s_a, s_b), (mx_a, mx_b), (p_a, p_b), (a_a, a_b)))
    for pair in range(npairs):
        o_ref[0, :, pair * LANES:(pair + 1) * LANES] = _pair_output(
            pair, l_ref, acc_ref).astype(o_ref.dtype)


def _dsa(qd, qi, small, kd, vdt, kk, tab):
    B, S, W = qd.shape
    t = ATT_T
    nk = S // t
    return pl.pallas_call(
        _dsa_kernel,
        out_shape=jax.ShapeDtypeStruct((B, S, W), BF16),
        grid=(B, nk),
        in_specs=[pl.BlockSpec((1, t, W), lambda b, i: (b, i, 0)),
                  pl.BlockSpec((1, t, qi.shape[2]), lambda b, i: (b, i, 0)),
                  pl.BlockSpec((1, SUBLANES, t), lambda b, i: (b, 1, i)),
                  _resident((1, S, W), lambda b, i: (b, 0, 0)),
                  _resident((nk, W, t), lambda b, i: (b, 0, 0)),
                  _resident((1, S, LANES), lambda b, i: (b, 0, 0)),
                  _resident(tab.shape, lambda b, i: (0, 0, 0, 0))],
        out_specs=pl.BlockSpec((1, t, W), lambda b, i: (b, i, 0)),
        scratch_shapes=[pltpu.VMEM((nk, t, t), F32),
                        pltpu.VMEM((nk, t, t), BF16),
                        pltpu.VMEM((2, t), F32),
                        pltpu.VMEM((N_DSA_HEADS, t, LANES), BF16),
                        pltpu.VMEM((N_IDX_HEADS, t, LANES), BF16)]
                       + _attention_scratch(t, N_DSA_HEADS),
        compiler_params=pltpu.CompilerParams(
            dimension_semantics=("arbitrary", "arbitrary"),
            vmem_limit_bytes=VMEM_LIMIT),
        name="dsa",
    )(qd, qi, small, kd, vdt, kk, tab)


def _post_kernel(x_ref, yf_ref, yd_ref, sgf_ref, sgd_ref, ada_ref, g2_ref, gf_ref,
                 wbf_ref, wbd_ref, wo_ref, w1_ref, w2_ref, o_ref):
    bf = jnp.dot(yf_ref[...], wbf_ref[...], preferred_element_type=F32)
    bd = jnp.dot(yd_ref[...], wbd_ref[...], preferred_element_type=F32)
    merged = sgf_ref[...].astype(F32) * bf + sgd_ref[...].astype(F32) * bd
    o = jnp.dot(merged.astype(BF16), wo_ref[...], preferred_element_type=F32)
    x1 = x_ref[...] + ada_ref[0, 2:3, :] * o
    h2 = _rms_mod(x1, g2_ref[...], ada_ref[0, 4:5, :], ada_ref[0, 3:4, :]).astype(BF16)
    acc = jnp.zeros(x1.shape, F32)
    for k in range(w1_ref.shape[1] // FF_CHUNK):
        u = jnp.dot(h2, w1_ref[:, k * FF_CHUNK:(k + 1) * FF_CHUNK], preferred_element_type=F32)
        u = jnp.square(jnp.maximum(u, 0.0)).astype(BF16)
        acc = acc + jnp.dot(u, w2_ref[k * FF_CHUNK:(k + 1) * FF_CHUNK, :],
                            preferred_element_type=F32)
    x2 = x1 + ada_ref[0, 5:6, :] * acc
    y = x2 * lax.rsqrt(jnp.mean(x2 * x2, axis=-1, keepdims=True) + EPS) * gf_ref[...]
    o_ref[...] = y


def _post(x2d, yf, yd, sgf, sgd, ada, g2, gfin, wbf, wbd, wo, w1, w2, S):
    M, D = x2d.shape
    tm = POST_TM
    tpb = S // tm
    rows = lambda w: pl.BlockSpec((tm, w), lambda i: (i, 0))
    full = lambda a: _resident(a.shape, lambda i: (0,) * a.ndim)
    return pl.pallas_call(
        _post_kernel,
        out_shape=jax.ShapeDtypeStruct((M, D), F32),
        grid=(M // tm,),
        in_specs=[rows(D), rows(yf.shape[1]), rows(yd.shape[1]), rows(D), rows(D),
                  pl.BlockSpec((1,) + ada.shape[1:], lambda i: (i // tpb, 0, 0)),
                  pl.BlockSpec((1, D), lambda i: (0, 0)),
                  pl.BlockSpec((1, D), lambda i: (0, 0)),
                  full(wbf), full(wbd), full(wo), full(w1), full(w2)],
        out_specs=rows(D),
        compiler_params=pltpu.CompilerParams(dimension_semantics=("arbitrary",),
                                             vmem_limit_bytes=VMEM_LIMIT),
        name="post",
    )(x2d, yf, yd, sgf, sgd, ada, g2, gfin, wbf, wbd, wo, w1, w2)


def kernel(x, c, w_ada, b_ada, g_norm1, w_in, b_forget, rel_bias, w_branch_fox,
           w_branch_dsa, w_out, g_norm2, w_mlp1, w_mlp2, g_final):
    B, S, D = x.shape
    fw = N_FOX_HEADS * HEAD_DIM
    dw = N_DSA_HEADS * HEAD_DIM
    iw = N_IDX_HEADS * IDX_DIM
    sizes = [fw, fw, fw, N_FOX_HEADS, dw, dw, dw, iw, IDX_DIM, N_IDX_HEADS, D, D]
    offs = [0]
    for s_ in sizes:
        offs.append(offs[-1] + s_)
    seg = lambda k: w_in[:, offs[k]:offs[k + 1]]
    q_scale = HEAD_DIM ** -0.5 * LOG2E
    i_scale = IDX_DIM ** -0.5
    w_main = jnp.concatenate(
        [seg(0) * q_scale, seg(1), seg(4) * q_scale, seg(5),
         seg(7) * i_scale, seg(8), seg(8), seg(10), seg(11)], axis=1).astype(BF16)
    w_t = jnp.concatenate([seg(2), seg(6)], axis=1).T.astype(BF16)
    w_small_t = (jnp.zeros((SMALL_ROWS, D), F32)
                 .at[:N_FOX_HEADS].set(seg(3).T)
                 .at[SUBLANES:SUBLANES + N_IDX_HEADS].set(seg(9).T)).astype(BF16)

    ada = _ada(c, w_ada, b_ada)
    tab = _biastab(rel_bias, BIAS_T)
    x2d = x.reshape(B * S, D)
    (qf, kf, qd, kd, qi, kk, sgf, sgd, vft, vdt, small) = _proj(
        x2d, ada, g_norm1.reshape(1, D), w_main, w_t, w_small_t, B, S)
    fk = _fcum(small, b_forget)
    r3 = lambda a: a.reshape(B, S, a.shape[-1])
    y_fox = _fox(r3(qf), r3(kf), vft, fk)
    y_dsa = _dsa(r3(qd), r3(qi), small, r3(kd), vdt, r3(kk), tab)
    out = _post(x2d, y_fox.reshape(B * S, fw), y_dsa.reshape(B * S, dw), sgf, sgd, ada,
                g_norm2.reshape(1, D), g_final.reshape(1, D),
                w_branch_fox.astype(BF16), w_branch_dsa.astype(BF16), w_out.astype(BF16),
                w_mlp1.astype(BF16), w_mlp2.astype(BF16), S)
    return out.reshape(B, S, D)
```
